```python
import math
import jax, jax.numpy as jnp
from jax import lax
import numpy as np

D_MODEL = 1024
BATCH = 2
SEQ = 8192
DEPTH = 2

D_MIX = D_MODEL
HEAD_DIM = 64
RWKV_WIDTH = D_MIX // 2
SB_WIDTH = D_MIX - RWKV_WIDTH
RWKV_HEADS = RWKV_WIDTH // HEAD_DIM
SB_HEADS = SB_WIDTH // HEAD_DIM
DECAY_LORA = 64
ICLR_LORA = 64
SHIFT_COLS = 3 * RWKV_WIDTH + DECAY_LORA + ICLR_LORA
RWKV_COLS = SHIFT_COLS + RWKV_WIDTH
SB_COLS = 4 * SB_WIDTH
IN_COLS = RWKV_COLS + SB_COLS
Q_BLOCK = 128
NORM_EPS = 1e-6
GN_EPS = 64e-5

kernel_name = "hymba_rwkv7_stickbreaking_adaln"


def rmsnorm(x, g):
    xf = x.astype(jnp.float32)
    y = xf * lax.rsqrt(jnp.mean(xf * xf, axis=-1, keepdims=True) + NORM_EPS)
    return (y * g.astype(jnp.float32)).astype(x.dtype)


def rwkv7_group(p, mu, w0, w2, a0, a2, k_k, k_a, r_k, ln_w, ln_b):
    B, S, _ = p.shape
    H, N = RWKV_HEADS, HEAD_DIM
    ps, g = p[..., :SHIFT_COLS], p[..., SHIFT_COLS:]
    prev = jnp.pad(ps, ((0, 0), (1, 0), (0, 0)))[:, :-1]
    ps = ps + (prev - ps) * mu
    r, k, v, zw, za = jnp.split(
        ps, [RWKV_WIDTH, 2 * RWKV_WIDTH, 3 * RWKV_WIDTH, 3 * RWKV_WIDTH + DECAY_LORA], axis=-1)
    w_log = -jax.nn.softplus(-(w0 + jnp.tanh(zw) @ w2)) - 0.5
    decay = jnp.exp(-jnp.exp(w_log.astype(jnp.float32)))
    a = jax.nn.sigmoid(a0 + za @ a2)
    hd = lambda t: t.reshape(B, S, H, N)
    r, k, v, decay, a = hd(r), hd(k), hd(v), hd(decay), hd(a)
    kk = hd(ps[..., RWKV_WIDTH:2 * RWKV_WIDTH] * k_k)
    kkf = kk.astype(jnp.float32)
    kk = kkf * lax.rsqrt(jnp.maximum(jnp.sum(kkf * kkf, axis=-1, keepdims=True), 1e-24))
    k = k * (1.0 + (a - 1.0) * k_a.reshape(H, N))

    def step(state, inp):
        r_t, w_t, k_t, v_t, kk_t, b_t = inp
        sa = jnp.einsum('bhij,bhj->bhi', state, -kk_t)
        state = (state * w_t[:, :, None, :]
                 + sa[..., None] * b_t[:, :, None, :]
                 + v_t[..., None] * k_t[:, :, None, :])
        y_t = jnp.einsum('bhij,bhj->bhi', state, r_t)
        return state, y_t

    tm = lambda t: jnp.moveaxis(t.astype(jnp.float32), 1, 0)
    state0 = jnp.zeros((B, H, N, N), jnp.float32)
    _, y = lax.scan(step, state0, (tm(r), tm(decay), tm(k), tm(v), tm(kk), tm(kk * a)))
    y = jnp.moveaxis(y, 0, 1)
    mean = jnp.mean(y, axis=-1, keepdims=True)
    var = jnp.mean(jnp.square(y - mean), axis=-1, keepdims=True)
    y = (y - mean) * lax.rsqrt(var + GN_EPS)
    y = y * ln_w.reshape(H, N) + ln_b.reshape(H, N)
    rf, kf, vf = r.astype(jnp.float32), k.astype(jnp.float32), v.astype(jnp.float32)
    y = y + jnp.sum(rf * kf * r_k, axis=-1, keepdims=True) * vf
    y = y.reshape(B, S, RWKV_WIDTH).astype(p.dtype)
    return y * jax.nn.silu(g)


def stick_breaking_group(p, sb_g):
    B, S, _ = p.shape
    H, N = SB_HEADS, HEAD_DIM
    n_blocks = S // Q_BLOCK
    q, k, v, g = jnp.split(p, 4, axis=-1)
    hd = lambda t: t.reshape(B, S, H, N).transpose(0, 2, 1, 3)
    q, k, v = hd(q), hd(k), hd(v)
    scale = 1.0 / math.sqrt(N)
    k_pos = jnp.arange(S)
    q_blocks = q.reshape(B, H, n_blocks, Q_BLOCK, N).transpose(2, 0, 1, 3, 4)

    def block(args):
        q_blk, idx = args
        q_pos = idx * Q_BLOCK + jnp.arange(Q_BLOCK)
        z = jnp.einsum('bhqd,bhkd->bhqk', q_blk, k).astype(jnp.float32) * scale
        mask = k_pos[None, :] < q_pos[:, None]
        log_1m = jnp.where(mask, jax.nn.log_sigmoid(-z), 0.0)
        after = lax.cumsum(log_1m, axis=3, reverse=True) - log_1m
        attn = jnp.where(mask, jnp.exp(jax.nn.log_sigmoid(z) + after), 0.0)
        return jnp.einsum('bhqk,bhkd->bhqd', attn.astype(v.dtype), v)

    o = lax.map(block, (q_blocks, jnp.arange(n_blocks)))
    o = o.transpose(1, 3, 0, 2, 4).reshape(B, S, H, N)
    o = rmsnorm(o, sb_g.reshape(H, N)).reshape(B, S, SB_WIDTH)
    return o * jax.nn.silu(g)


def setup_inputs(seed: int = 0) -> dict:
    key = jax.random.key(seed)
    ks = jax.random.split(key, 20)
    nrm = lambda k, shape, s: s * jax.random.normal(k, shape, jnp.float32)
    return {
        "x": nrm(ks[0], (BATCH, SEQ, D_MODEL), 1.0),
        "c": nrm(ks[1], (BATCH, D_MODEL), 1.0),
        "norm_g": 1.0 + nrm(ks[2], (DEPTH, D_MODEL), 0.05),
        "ada_w": nrm(ks[3], (DEPTH, D_MODEL, 3 * D_MODEL), 0.5 * D_MODEL ** -0.5),
        "ada_b": nrm(ks[4], (DEPTH, 3 * D_MODEL), 0.02),
        "w_in": nrm(ks[5], (DEPTH, D_MODEL, IN_COLS), D_MODEL ** -0.5),
        "w_out": nrm(ks[6], (DEPTH, D_MIX, D_MODEL), D_MIX ** -0.5),
        "tshift_mu": jax.random.uniform(ks[7], (DEPTH, SHIFT_COLS), jnp.float32),
        "decay_w0": jax.random.uniform(ks[8], (DEPTH, RWKV_WIDTH), jnp.float32, minval=-4.0, maxval=1.0),
        "decay_w2": nrm(ks[9], (DEPTH, DECAY_LORA, RWKV_WIDTH), 0.5 * DECAY_LORA ** -0.5),
        "iclr_a0": nrm(ks[10], (DEPTH, RWKV_WIDTH), 0.5),
        "iclr_a2": nrm(ks[11], (DEPTH, ICLR_LORA, RWKV_WIDTH), 0.5 * ICLR_LORA ** -0.5),
        "k_k": 0.85 + nrm(ks[12], (DEPTH, RWKV_WIDTH), 0.05),
        "k_a": 1.0 + nrm(ks[13], (DEPTH, RWKV_WIDTH), 0.05),
        "r_k": nrm(ks[14], (DEPTH, RWKV_HEADS, HEAD_DIM), 0.1),
        "rwkv_ln_w": 1.0 + nrm(ks[15], (DEPTH, RWKV_WIDTH), 0.05),
        "rwkv_ln_b": nrm(ks[16], (DEPTH, RWKV_WIDTH), 0.02),
        "sb_norm_g": 1.0 + nrm(ks[17], (DEPTH, SB_WIDTH), 0.05),
        "final_g": 1.0 + nrm(ks[18], (D_MODEL,), 0.05),
    }


def reference(x, c, norm_g, ada_w, ada_b, w_in, w_out, tshift_mu, decay_w0, decay_w2,
              iclr_a0, iclr_a2, k_k, k_a, r_k, rwkv_ln_w, rwkv_ln_b, sb_norm_g, final_g):
    c_act = jax.nn.silu(c)
    for l in range(DEPTH):
        mod = c_act @ ada_w[l] + ada_b[l]
        shift, scale, gate = jnp.split(mod, 3, axis=-1)
        h = rmsnorm(x, norm_g[l]) * (1.0 + scale[:, None, :]) + shift[:, None, :]
        p = h @ w_in[l]
        y_rwkv = rwkv7_group(p[..., :RWKV_COLS], tshift_mu[l], decay_w0[l], decay_w2[l],
                             iclr_a0[l], iclr_a2[l], k_k[l], k_a[l], r_k[l],
                             rwkv_ln_w[l], rwkv_ln_b[l])
        y_sb = stick_breaking_group(p[..., RWKV_COLS:], sb_norm_g[l])
        y = jnp.concatenate([y_rwkv, y_sb], axis=-1) @ w_out[l]
        x = x + gate[:, None, :] * y
    return rmsnorm(x, final_g)
```

```python
import functools

import jax
import jax.numpy as jnp
from jax import lax
from jax.experimental import pallas as pl
from jax.experimental.pallas import tpu as pltpu

F32 = jnp.float32
BF16 = jnp.bfloat16

HEAD_DIM = 64
LANES = 128
HEADS_PER_BLOCK = LANES // HEAD_DIM
RWKV_CHUNK = 64
SB_BLOCK = 128
DECAY_LORA = 64
ICLR_LORA = 64
NORM_EPS = 1e-6
GN_EPS = 64e-5
VMEM_LIMIT_BYTES = 56 * 1024 * 1024

NN = (((1,), (0,)), ((), ()))
NT = (((1,), (1,)), ((), ()))
TN = (((0,), (0,)), ((), ()))


def _dot(a, b, dims=NN):
    return lax.dot_general(a, b, dims, preferred_element_type=F32)


def _split2(x):
    hi = x.astype(BF16)
    lo = (x - hi.astype(F32)).astype(BF16)
    return hi, lo


def _split3(x):
    hi = x.astype(BF16)
    r1 = x - hi.astype(F32)
    mid = r1.astype(BF16)
    lo = (r1 - mid.astype(F32)).astype(BF16)
    return hi, mid, lo


def _dot_exact_rhs(x, rhs_bf16, dims=NN):
    hi, mid, lo = _split3(x)
    return _dot(hi, rhs_bf16, dims) + _dot(mid, rhs_bf16, dims) + _dot(lo, rhs_bf16, dims)


def _dot_x3(a, b_hi, b_lo):
    a_hi, a_lo = _split2(a)
    return _dot(a_hi, b_hi) + _dot(a_lo, b_hi) + _dot(a_hi, b_lo)


def _softplus(u):
    return jnp.maximum(u, 0.0) + jnp.log(1.0 + jnp.exp(-jnp.abs(u)))


def _sigmoid(u):
    return 1.0 / (1.0 + jnp.exp(-u))


def _silu(u):
    return u * _sigmoid(u)


def _head_ones():
    r = lax.broadcasted_iota(jnp.int32, (LANES, LANES), 0)
    c = lax.broadcasted_iota(jnp.int32, (LANES, LANES), 1)
    return jnp.where((r >= HEAD_DIM) == (c >= HEAD_DIM), 1.0, 0.0).astype(BF16)


def _ada_kernel(c_ref, w_ref, b_ref, o_ref):
    c_act = _silu(c_ref[...])
    w_hi, w_lo = _split2(w_ref[0])
    o_ref[0] = _dot_x3(c_act, w_hi, w_lo) + b_ref[0]


def _ada_mod(c, ada_w, ada_b):
    depth, d, d3 = ada_w.shape
    b = c.shape[0]
    rows = 8
    c8 = jnp.zeros((rows, d), F32).at[:b].set(c)
    out = pl.pallas_call(
        _ada_kernel,
        grid=(depth, d3 // d),
        in_specs=[
            pl.BlockSpec((rows, d), lambda l, j: (0, 0)),
            pl.BlockSpec((1, d, d), lambda l, j: (l, 0, j)),
            pl.BlockSpec((1, 1, d), lambda l, j: (l, 0, j)),
        ],
        out_specs=pl.BlockSpec((1, rows, d), lambda l, j: (l, 0, j)),
        out_shape=jax.ShapeDtypeStruct((depth, rows, d3), F32),
        name="ada_mod",
    )(c8, ada_w, ada_b.reshape(depth, 1, d3))
    return out[:, :b]


def _in_proj_kernel(x_ref, g_ref, sc_ref, sh_ref, w_ref, rw_ref, zz_ref, sbq_ref, sbg_ref, *, widths):
    rw_w, zz_w, sbq_w, sbg_w = widths
    x = x_ref[0]
    ms = jnp.mean(x * x, axis=-1, keepdims=True)
    h = x * lax.rsqrt(ms + NORM_EPS) * g_ref[...]
    h = h * (1.0 + sc_ref[0]) + sh_ref[0]
    hb = h.astype(BF16)

    def emit(out_ref, out_col, w_col, width):
        step = 512
        for c0 in range(0, width, step):
            cw = min(step, width - c0)
            res = _dot(hb, w_ref[:, w_col + c0:w_col + c0 + cw])
            out_ref[0, :, out_col + c0:out_col + c0 + cw] = res.astype(out_ref.dtype)

    shift_rkv = rw_w - sbg_w
    emit(rw_ref, 0, 0, shift_rkv)
    emit(zz_ref, 0, shift_rkv, zz_w)
    emit(rw_ref, shift_rkv, shift_rkv + zz_w, sbg_w)
    sb0 = rw_w + zz_w
    emit(sbq_ref, 0, sb0, sbq_w)
    emit(sbg_ref, 0, sb0 + sbq_w, sbg_w)


def _in_proj(x, g, scale, shift, w_bf16, tm):
    b, s, d = x.shape
    width = d // 2
    widths = (4 * width, DECAY_LORA + ICLR_LORA, 3 * width, width)
    n_cols = w_bf16.shape[1]
    assert sum(widths) == n_cols
    return pl.pallas_call(
        functools.partial(_in_proj_kernel, widths=widths),
        grid=(b, s // tm),
        in_specs=[
            pl.BlockSpec((1, tm, d), lambda i, j: (i, j, 0)),
            pl.BlockSpec((1, d), lambda i, j: (0, 0)),
            pl.BlockSpec((1, 1, d), lambda i, j: (i, 0, 0)),
            pl.BlockSpec((1, 1, d), lambda i, j: (i, 0, 0)),
            pl.BlockSpec((d, n_cols), lambda i, j: (0, 0)),
        ],
        out_specs=[
            pl.BlockSpec((1, tm, widths[0]), lambda i, j: (i, j, 0)),
            pl.BlockSpec((1, tm, widths[1]), lambda i, j: (i, j, 0)),
            pl.BlockSpec((1, tm, widths[2]), lambda i, j: (i, j, 0)),
            pl.BlockSpec((1, tm, widths[3]), lambda i, j: (i, j, 0)),
        ],
        out_shape=[
            jax.ShapeDtypeStruct((b, s, widths[0]), F32),
            jax.ShapeDtypeStruct((b, s, widths[1]), F32),
            jax.ShapeDtypeStruct((b, s, widths[2]), BF16),
            jax.ShapeDtypeStruct((b, s, widths[3]), F32),
        ],
        compiler_params=pltpu.CompilerParams(
            dimension_semantics=("parallel", "parallel"), vmem_limit_bytes=VMEM_LIMIT_BYTES),
        name="in_proj",
    )(x, g.reshape(1, d), scale.reshape(b, 1, d), shift.reshape(b, 1, d), w_bf16)


def _rwkv_kernel(rkv_ref, zz_ref, g_ref, mu_ref, muz_ref, wl_hi_ref, wl_lo_ref, vec_ref, y_ref,
                 state_ref, prev_ref, prevz_ref, *, n_blocks):
    L = RWKV_CHUNK
    width = n_blocks * LANES
    t = pl.program_id(1)

    @pl.when(t == 0)
    def _():
        state_ref[...] = jnp.zeros_like(state_ref)
        prev_ref[...] = jnp.zeros_like(prev_ref)
        prevz_ref[...] = jnp.zeros_like(prevz_ref)

    def token_shift(x, carry_ref, mu):
        row = lax.broadcasted_iota(jnp.int32, x.shape, 0)
        prev = jnp.where(row == 0, carry_ref[7:8, :], pltpu.roll(x, 1, 0))
        carry_ref[...] = x[L - 8:L]
        return x + (prev - x) * mu

    xs = token_shift(rkv_ref[0], prev_ref, mu_ref[...])
    zs = token_shift(zz_ref[0], prevz_ref, muz_ref[...])

    k_k = vec_ref[0:1, :]
    k_a = vec_ref[1:2, :]
    r_k = vec_ref[2:3, :]
    ln_w = vec_ref[3:4, :]
    ln_b = vec_ref[4:5, :]
    w0 = vec_ref[5:6, :]
    a0 = vec_ref[6:7, :]

    lane_z = lax.broadcasted_iota(jnp.int32, zs.shape, 1)
    act = jnp.where(lane_z < DECAY_LORA, jnp.tanh(zs), zs)
    lora = _dot_x3(act, wl_hi_ref[...], wl_lo_ref[...])
    w_log = -_softplus(-(w0 + lora[:, :width])) - 0.5
    log_decay = -jnp.exp(w_log)
    iclr = _sigmoid(a0 + lora[:, width:])

    head_ones = _head_ones()
    lane = lax.broadcasted_iota(jnp.int32, (L, LANES), 1)
    head0 = lane < HEAD_DIM
    tr = lax.broadcasted_iota(jnp.int32, (L, L), 0)
    tc = lax.broadcasted_iota(jnp.int32, (L, L), 1)
    cum_incl = jnp.where(tc <= tr, 1.0, 0.0).astype(BF16)
    br = lax.broadcasted_iota(jnp.int32, (2 * L, 2 * L), 0)
    bc = lax.broadcasted_iota(jnp.int32, (2 * L, 2 * L), 1)
    same_head = (br >= L) == (bc >= L)
    strict = same_head & (bc < br)
    incl = same_head & (bc <= br)
    eye = jnp.where(br == bc, 1.0, 0.0)

    def stack_masked(x):
        return jnp.concatenate([jnp.where(head0, x, 0.0), jnp.where(head0, 0.0, x)], axis=0)

    def stack_plain(x):
        return jnp.concatenate([x, x], axis=0)

    def head_sum(x):
        return _dot_exact_rhs(x, head_ones)

    for p in range(n_blocks):
        sl = slice(p * LANES, (p + 1) * LANES)
        r = xs[:, sl]
        k = xs[:, width + p * LANES:width + (p + 1) * LANES]
        v = xs[:, 2 * width + p * LANES:2 * width + (p + 1) * LANES]
        lw = log_decay[:, sl]
        a = iclr[:, sl]

        kk = k * k_k[:, sl]
        kk = kk * lax.rsqrt(jnp.maximum(head_sum(kk * kk), 1e-24))
        kmod = k * (1.0 + (a - 1.0) * k_a[:, sl])
        bvec = kk * a

        c = _dot_cumsum(cum_incl, lw)
        c_last = c[L - 1:L, :]
        g_in = jnp.exp(c)
        g_ex = jnp.exp(c - lw)
        g_inv = jnp.exp(-c)
        g_rem = jnp.exp(c_last - c)

        xa = stack_masked(-kk * g_ex)
        xr = stack_masked(r * g_in)
        yb = stack_plain(bvec * g_inv)
        yk = stack_plain(kmod * g_inv)
        vbd = stack_masked(v)
        xbg = stack_masked(bvec * g_rem)
        xkg = stack_masked(kmod * g_rem)

        gram = _dot(jnp.concatenate([xa, xr], axis=0).astype(BF16),
                    jnp.concatenate([yb, yk], axis=0).astype(BF16), NT)
        a_ab = jnp.where(strict, gram[:2 * L, :2 * L], 0.0)
        a_ak = jnp.where(strict, gram[:2 * L, 2 * L:], 0.0)
        a_rb = jnp.where(incl, gram[2 * L:, :2 * L], 0.0)
        a_rk = jnp.where(incl, gram[2 * L:, 2 * L:], 0.0)

        n_sq = L.bit_length() - 1
        nb = a_ab.astype(BF16)
        pmat = eye + a_ab
        qmat = _dot(nb, nb)
        for step in range(1, n_sq):
            qb = qmat.astype(BF16)
            if step < n_sq - 1:
                both = _dot(jnp.concatenate([pmat, qmat], axis=0).astype(BF16), qb)
                pmat, qmat = pmat + both[:2 * L], both[2 * L:]
            else:
                pmat = pmat + _dot(pmat.astype(BF16), qb)
        tmat = pmat

        w1 = _dot(a_ak.astype(BF16), vbd.astype(BF16))
        aw = _dot(tmat.astype(BF16), jnp.concatenate([xa, w1], axis=1).astype(BF16))
        a_hat, w_hat = aw[:, :LANES], aw[:, LANES:]

        h_t = state_ref[p]
        s9 = _dot(jnp.concatenate([a_hat, xr], axis=0).astype(BF16), h_t.astype(BF16), NT)
        u = s9[:2 * L] + w_hat
        uv = jnp.concatenate([u, vbd], axis=0).astype(BF16)
        y_bd = s9[2 * L:] + _dot(jnp.concatenate([a_rb, a_rk], axis=1).astype(BF16), uv)
        state_ref[p] = h_t * jnp.exp(c_last) + _dot(
            uv, jnp.concatenate([xbg, xkg], axis=0).astype(BF16), TN)

        y = y_bd[:L] + y_bd[L:]
        mean = head_sum(y) * (1.0 / HEAD_DIM)
        dlt = y - mean
        var = head_sum(dlt * dlt) * (1.0 / HEAD_DIM)
        yn = dlt * lax.rsqrt(var + GN_EPS) * ln_w[:, sl] + ln_b[:, sl]
        bonus = head_sum(r * kmod * r_k[:, sl]) * v
        gate = g_ref[0, :, sl]
        y_ref[0, :, sl] = ((yn + bonus) * _silu(gate)).astype(y_ref.dtype)


def _dot_cumsum(cum_incl, x):
    hi, mid, lo = _split3(x)
    return _dot(cum_incl, hi) + _dot(cum_incl, mid) + _dot(cum_incl, lo)


def _rwkv(rw, zz, mu, w0, w2, a0, a2, k_k, k_a, r_k, ln_w, ln_b):
    b, s, w4 = rw.shape
    width = w4 // 4
    n_blocks = width // LANES
    L = RWKV_CHUNK
    mu_rkv = mu[:3 * width].reshape(1, 3 * width)
    mu_z = mu[3 * width:].reshape(1, DECAY_LORA + ICLR_LORA)
    w_lora = jnp.zeros((DECAY_LORA + ICLR_LORA, 2 * width), F32)
    w_lora = w_lora.at[:DECAY_LORA, :width].set(w2).at[DECAY_LORA:, width:].set(a2)
    wl_hi = w_lora.astype(BF16)
    wl_lo = (w_lora - wl_hi.astype(F32)).astype(BF16)
    vec = jnp.stack([k_k, k_a, r_k.reshape(width), ln_w, ln_b, w0, a0, jnp.zeros_like(w0)], axis=0)
    return pl.pallas_call(
        functools.partial(_rwkv_kernel, n_blocks=n_blocks),
        grid=(b, s // L),
        in_specs=[
            pl.BlockSpec((1, L, 3 * width), lambda i, j: (i, j, 0)),
            pl.BlockSpec((1, L, LANES), lambda i, j: (i, j, 0)),
            pl.BlockSpec((1, L, width), lambda i, j: (i, j, 3)),
            pl.BlockSpec((1, 3 * width), lambda i, j: (0, 0)),
            pl.BlockSpec((1, LANES), lambda i, j: (0, 0)),
            pl.BlockSpec((LANES, 2 * width), lambda i, j: (0, 0)),
            pl.BlockSpec((LANES, 2 * width), lambda i, j: (0, 0)),
            pl.BlockSpec((8, width), lambda i, j: (0, 0)),
        ],
        out_specs=pl.BlockSpec((1, L, width), lambda i, j: (i, j, 0)),
        out_shape=jax.ShapeDtypeStruct((b, s, width), BF16),
        scratch_shapes=[
            pltpu.VMEM((n_blocks, LANES, LANES), F32),
            pltpu.VMEM((8, 3 * width), F32),
            pltpu.VMEM((8, LANES), F32),
        ],
        compiler_params=pltpu.CompilerParams(
            dimension_semantics=("parallel", "arbitrary"), vmem_limit_bytes=VMEM_LIMIT_BYTES),
        name="rwkv7",
    )(rw, zz, rw, mu_rkv, mu_z, wl_hi, wl_lo, vec)


def _sb_kernel(q_ref, k_ref, v_ref, sg_ref, o_ref, acc_ref, carry_ref):
    T = SB_BLOCK
    qi = pl.program_id(2)
    q = q_ref[0]
    lane = lax.broadcasted_iota(jnp.int32, (T, LANES), 1)
    head0 = lane < HEAD_DIM
    zero = jnp.zeros_like(q)
    scale = HEAD_DIM ** -0.5
    qs = jnp.concatenate([jnp.where(head0, q, zero), jnp.where(head0, zero, q)], axis=0) * scale

    sr = lax.broadcasted_iota(jnp.int32, (T, T), 0)
    sc = lax.broadcasted_iota(jnp.int32, (T, T), 1)
    tri_ones = jnp.concatenate([jnp.where(sr > sc, 1.0, 0.0), jnp.ones((T, T), F32)], axis=1).astype(BF16)
    mr = lax.broadcasted_iota(jnp.int32, (2 * T, T), 0)
    mc = lax.broadcasted_iota(jnp.int32, (2 * T, T), 1)
    causal = mc < jnp.where(mr >= T, mr - T, mr)

    acc_ref[...] = jnp.zeros_like(acc_ref)
    carry_ref[...] = jnp.zeros_like(carry_ref)

    def tile(j, diag):
        start = pl.multiple_of(j * T, T)
        kb = k_ref[0, pl.ds(start, T), :]
        vb = v_ref[0, pl.ds(start, T), :]
        z = _dot(qs, kb, NT)
        soft = jnp.log(1.0 + jnp.exp(-jnp.abs(z)))
        log_1m = -(jnp.maximum(z, 0.0) + soft)
        log_b = jnp.minimum(z, 0.0) - soft
        if diag:
            log_1m = jnp.where(causal, log_1m, 0.0)
        hi, lo = _split2(log_1m)
        sums = _dot(hi, tri_ones) + _dot(lo, tri_ones)
        carry = carry_ref[...]
        attn = jnp.exp(log_b + sums[:, :T] + carry)
        if diag:
            attn = jnp.where(causal, attn, 0.0)
        acc_ref[...] += _dot(attn.astype(BF16), vb)
        carry_ref[...] = carry + sums[:, T:]

    tile(qi, True)

    def body(it, _):
        tile(qi - 1 - it, False)
        return 0

    lax.fori_loop(0, qi, body, 0)

    acc = acc_ref[...]
    o = jnp.where(head0, acc[:T], acc[T:])
    ms = _dot_exact_rhs(o * o, _head_ones()) * (1.0 / HEAD_DIM)
    o_ref[0] = o * lax.rsqrt(ms + NORM_EPS) * sg_ref[...]


def _stick_breaking(qkv, sb_g):
    b, s, w3 = qkv.shape
    width = w3 // 3
    n_blocks = width // LANES
    T = SB_BLOCK
    return pl.pallas_call(
        _sb_kernel,
        grid=(b, n_blocks, s // T),
        in_specs=[
            pl.BlockSpec((1, T, LANES), lambda i, p, j: (i, j, p)),
            pl.BlockSpec((1, s, LANES), lambda i, p, j: (i, 0, n_blocks + p)),
            pl.BlockSpec((1, s, LANES), lambda i, p, j: (i, 0, 2 * n_blocks + p)),
            pl.BlockSpec((1, LANES), lambda i, p, j: (0, p)),
        ],
        out_specs=pl.BlockSpec((1, T, LANES), lambda i, p, j: (i, j, p)),
        out_shape=jax.ShapeDtypeStruct((b, s, width), F32),
        scratch_shapes=[
            pltpu.VMEM((2 * T, LANES), F32),
            pltpu.VMEM((2 * T, T), F32),
        ],
        compiler_params=pltpu.CompilerParams(
            dimension_semantics=("parallel", "parallel", "arbitrary"), vmem_limit_bytes=VMEM_LIMIT_BYTES),
        name="stick_breaking",
    )(qkv, qkv, qkv, sb_g.reshape(1, width))


def _out_proj_kernel(yr_ref, os_ref, sg_ref, w_ref, x_ref, gate_ref, fg_ref, o_ref, perm_ref, *, final_norm):
    half = yr_ref.shape[-1]
    n_qb, sub = os_ref.shape[1], os_ref.shape[2]
    o_tok = os_ref[0].reshape(n_qb * sub, half)
    cols = []
    for cb in range(half // LANES):
        perm_ref[cb] = o_tok[:, cb * LANES:(cb + 1) * LANES]
        cols.append(jnp.concatenate(
            [perm_ref[cb, pl.ds(ii, n_qb, stride=sub), :] for ii in range(sub)], axis=0))
    o_rows = jnp.concatenate(cols, axis=1)
    y_sb = (o_rows * _silu(sg_ref[0])).astype(BF16)
    y = _dot(yr_ref[0], w_ref[:half, :]) + _dot(y_sb, w_ref[half:, :])
    x = x_ref[0] + gate_ref[0] * y
    if final_norm:
        ms = jnp.mean(x * x, axis=-1, keepdims=True)
        x = x * lax.rsqrt(ms + NORM_EPS) * fg_ref[...]
    o_ref[0] = x


def _out_proj(y_rw, o_sb, sb_gate, w_bf16, x, gate, final_g, final_norm):
    b, s, d = x.shape
    half = y_rw.shape[-1]
    n_qb = s // SB_BLOCK
    sub = 8
    tm = sub * n_qb
    o_sb4 = o_sb.reshape(b, n_qb, SB_BLOCK, half)
    return pl.pallas_call(
        functools.partial(_out_proj_kernel, final_norm=final_norm),
        grid=(b, s // tm),
        in_specs=[
            pl.BlockSpec((1, tm, half), lambda i, j: (i, j, 0)),
            pl.BlockSpec((1, n_qb, sub, half), lambda i, j: (i, 0, j, 0)),
            pl.BlockSpec((1, tm, half), lambda i, j: (i, j, 0)),
            pl.BlockSpec((2 * half, d), lambda i, j: (0, 0)),
            pl.BlockSpec((1, tm, d), lambda i, j: (i, j, 0)),
            pl.BlockSpec((1, 1, d), lambda i, j: (i, 0, 0)),
            pl.BlockSpec((1, d), lambda i, j: (0, 0)),
        ],
        out_specs=pl.BlockSpec((1, tm, d), lambda i, j: (i, j, 0)),
        out_shape=jax.ShapeDtypeStruct((b, s, d), F32),
        scratch_shapes=[pltpu.VMEM((half // LANES, tm, LANES), F32)],
        compiler_params=pltpu.CompilerParams(
            dimension_semantics=("parallel", "parallel"), vmem_limit_bytes=VMEM_LIMIT_BYTES),
        name="out_proj",
    )(y_rw, o_sb4, sb_gate, w_bf16, x, gate.reshape(b, 1, d), final_g.reshape(1, d))


def _row_tile(s):
    for tm in (512, 256, 128, 64, 32, 16, 8):
        if s % tm == 0:
            return tm
    raise ValueError(f"sequence length {s} must be a multiple of 8")


def kernel(x, c, norm_g, ada_w, ada_b, w_in, w_out, tshift_mu, decay_w0, decay_w2, iclr_a0, iclr_a2,
           k_k, k_a, r_k, rwkv_ln_w, rwkv_ln_b, sb_norm_g, final_g):
    depth = norm_g.shape[0]
    b, s, d = x.shape
    assert s % SB_BLOCK == 0 and s % RWKV_CHUNK == 0
    tm = _row_tile(s)
    mod = _ada_mod(c, ada_w, ada_b)
    w_in_b = w_in.astype(BF16)
    w_out_b = w_out.astype(BF16)
    for l in range(depth):
        shift, scale, gate = mod[l, :, :d], mod[l, :, d:2 * d], mod[l, :, 2 * d:]
        rw, zz, sb_qkv, sb_gate = _in_proj(x, norm_g[l], scale, shift, w_in_b[l], tm)
        y_rw = _rwkv(rw, zz, tshift_mu[l], decay_w0[l], decay_w2[l], iclr_a0[l], iclr_a2[l],
                     k_k[l], k_a[l], r_k[l], rwkv_ln_w[l], rwkv_ln_b[l])
        o_sb = _stick_breaking(sb_qkv, sb_norm_g[l])
        x = _out_proj(y_rw, o_sb, sb_gate, w_out_b[l], x, gate, final_g, l == depth - 1)
    return x
```

```python
import functools

import jax
import jax.numpy as jnp
from jax import lax
from jax.experimental import pallas as pl
from jax.experimental.pallas import tpu as pltpu

F32 = jnp.float32
BF16 = jnp.bfloat16

HEAD_DIM = 64
LANES = 128
HEADS_PER_BLOCK = LANES // HEAD_DIM
RWKV_CHUNK = 64
SB_BLOCK = 128
SB_F32_EXP_UNDERFLOW = 106.0
DECAY_LORA = 64
ICLR_LORA = 64
NORM_EPS = 1e-6
GN_EPS = 64e-5
VMEM_LIMIT_BYTES = 56 * 1024 * 1024

NN = (((1,), (0,)), ((), ()))
NT = (((1,), (1,)), ((), ()))
TN = (((0,), (0,)), ((), ()))


def _dot(a, b, dims=NN):
    return lax.dot_general(a, b, dims, preferred_element_type=F32)


def _split2(x):
    hi = x.astype(BF16)
    lo = (x - hi.astype(F32)).astype(BF16)
    return hi, lo


def _split3(x):
    hi = x.astype(BF16)
    r1 = x - hi.astype(F32)
    mid = r1.astype(BF16)
    lo = (r1 - mid.astype(F32)).astype(BF16)
    return hi, mid, lo


def _dot_exact_rhs(x, rhs_bf16, dims=NN):
    hi, mid, lo = _split3(x)
    return _dot(hi, rhs_bf16, dims) + _dot(mid, rhs_bf16, dims) + _dot(lo, rhs_bf16, dims)


def _dot_x3(a, b_hi, b_lo):
    a_hi, a_lo = _split2(a)
    return _dot(a_hi, b_hi) + _dot(a_lo, b_hi) + _dot(a_hi, b_lo)


def _softplus(u):
    return jnp.maximum(u, 0.0) + jnp.log(1.0 + jnp.exp(-jnp.abs(u)))


def _sigmoid(u):
    return 1.0 / (1.0 + jnp.exp(-u))


def _silu(u):
    return u * _sigmoid(u)


def _head_ones():
    r = lax.broadcasted_iota(jnp.int32, (LANES, LANES), 0)
    c = lax.broadcasted_iota(jnp.int32, (LANES, LANES), 1)
    return jnp.where((r >= HEAD_DIM) == (c >= HEAD_DIM), 1.0, 0.0).astype(BF16)


def _ada_kernel(c_ref, w_ref, b_ref, o_ref):
    c_act = _silu(c_ref[...])
    w_hi, w_lo = _split2(w_ref[0])
    o_ref[0] = _dot_x3(c_act, w_hi, w_lo) + b_ref[0]


def _ada_mod(c, ada_w, ada_b):
    depth, d, d3 = ada_w.shape
    b = c.shape[0]
    rows = 8
    c8 = jnp.zeros((rows, d), F32).at[:b].set(c)
    out = pl.pallas_call(
        _ada_kernel,
        grid=(depth, d3 // d),
        in_specs=[
            pl.BlockSpec((rows, d), lambda l, j: (0, 0)),
            pl.BlockSpec((1, d, d), lambda l, j: (l, 0, j)),
            pl.BlockSpec((1, 1, d), lambda l, j: (l, 0, j)),
        ],
        out_specs=pl.BlockSpec((1, rows, d), lambda l, j: (l, 0, j)),
        out_shape=jax.ShapeDtypeStruct((depth, rows, d3), F32),
        name="ada_mod",
    )(c8, ada_w, ada_b.reshape(depth, 1, d3))
    return out[:, :b]


def _in_proj_kernel(x_ref, g_ref, sc_ref, sh_ref, w_ref, rw_ref, zz_ref, sbq_ref, sbg_ref, *, widths):
    rw_w, zz_w, sbq_w, sbg_w = widths
    x = x_ref[0]
    ms = jnp.mean(x * x, axis=-1, keepdims=True)
    h = x * lax.rsqrt(ms + NORM_EPS) * g_ref[...]
    h = h * (1.0 + sc_ref[0]) + sh_ref[0]
    hb = h.astype(BF16)

    def emit(out_ref, out_col, w_col, width):
        step = 512
        for c0 in range(0, width, step):
            cw = min(step, width - c0)
            res = _dot(hb, w_ref[:, w_col + c0:w_col + c0 + cw])
            out_ref[0, :, out_col + c0:out_col + c0 + cw] = res.astype(out_ref.dtype)

    shift_rkv = rw_w - sbg_w
    emit(rw_ref, 0, 0, shift_rkv)
    emit(zz_ref, 0, shift_rkv, zz_w)
    emit(rw_ref, shift_rkv, shift_rkv + zz_w, sbg_w)
    sb0 = rw_w + zz_w
    emit(sbq_ref, 0, sb0, sbq_w)
    emit(sbg_ref, 0, sb0 + sbq_w, sbg_w)


def _in_proj(x, g, scale, shift, w_bf16, tm):
    b, s, d = x.shape
    width = d // 2
    widths = (4 * width, DECAY_LORA + ICLR_LORA, 3 * width, width)
    n_cols = w_bf16.shape[1]
    assert sum(widths) == n_cols
    return pl.pallas_call(
        functools.partial(_in_proj_kernel, widths=widths),
        grid=(b, s // tm),
        in_specs=[
            pl.BlockSpec((1, tm, d), lambda i, j: (i, j, 0)),
            pl.BlockSpec((1, d), lambda i, j: (0, 0)),
            pl.BlockSpec((1, 1, d), lambda i, j: (i, 0, 0)),
            pl.BlockSpec((1, 1, d), lambda i, j: (i, 0, 0)),
            pl.BlockSpec((d, n_cols), lambda i, j: (0, 0)),
        ],
        out_specs=[
            pl.BlockSpec((1, tm, widths[0]), lambda i, j: (i, j, 0)),
            pl.BlockSpec((1, tm, widths[1]), lambda i, j: (i, j, 0)),
            pl.BlockSpec((1, tm, widths[2]), lambda i, j: (i, j, 0)),
            pl.BlockSpec((1, tm, widths[3]), lambda i, j: (i, j, 0)),
        ],
        out_shape=[
            jax.ShapeDtypeStruct((b, s, widths[0]), F32),
            jax.ShapeDtypeStruct((b, s, widths[1]), F32),
            jax.ShapeDtypeStruct((b, s, widths[2]), BF16),
            jax.ShapeDtypeStruct((b, s, widths[3]), F32),
        ],
        compiler_params=pltpu.CompilerParams(
            dimension_semantics=("parallel", "parallel"), vmem_limit_bytes=VMEM_LIMIT_BYTES),
        name="in_proj",
    )(x, g.reshape(1, d), scale.reshape(b, 1, d), shift.reshape(b, 1, d), w_bf16)


def _rwkv_kernel(rkv_ref, zz_ref, g_ref, mu_ref, muz_ref, wl_hi_ref, wl_lo_ref, vec_ref, y_ref,
                 state_ref, prev_ref, prevz_ref, *, n_blocks):
    L = RWKV_CHUNK
    n_batch = rkv_ref.shape[0]
    width = n_blocks * LANES
    t = pl.program_id(0)

    @pl.when(t == 0)
    def _():
        state_ref[...] = jnp.zeros_like(state_ref)
        prev_ref[...] = jnp.zeros_like(prev_ref)
        prevz_ref[...] = jnp.zeros_like(prevz_ref)

    def token_shift(x, carry_ref, b, mu):
        row = lax.broadcasted_iota(jnp.int32, x.shape, 0)
        prev = jnp.where(row == 0, carry_ref[b, 7:8, :], pltpu.roll(x, 1, 0))
        carry_ref[b] = x[L - 8:L]
        return x + (prev - x) * mu

    k_k = vec_ref[0:1, :]
    k_a = vec_ref[1:2, :]
    r_k = vec_ref[2:3, :]
    ln_w = vec_ref[3:4, :]
    ln_b = vec_ref[4:5, :]
    w0 = vec_ref[5:6, :]
    a0 = vec_ref[6:7, :]

    head_ones = _head_ones()
    lane = lax.broadcasted_iota(jnp.int32, (L, LANES), 1)
    head0 = lane < HEAD_DIM
    tr = lax.broadcasted_iota(jnp.int32, (L, L), 0)
    tc = lax.broadcasted_iota(jnp.int32, (L, L), 1)
    cum_incl = jnp.where(tc <= tr, 1.0, 0.0).astype(BF16)
    br = lax.broadcasted_iota(jnp.int32, (2 * L, 2 * L), 0)
    bc = lax.broadcasted_iota(jnp.int32, (2 * L, 2 * L), 1)
    same_head = (br >= L) == (bc >= L)
    strict = same_head & (bc < br)
    incl = same_head & (bc <= br)
    eye = jnp.where(br == bc, 1.0, 0.0)

    def stack_masked(x):
        return jnp.concatenate([jnp.where(head0, x, 0.0), jnp.where(head0, 0.0, x)], axis=0)

    def stack_plain(x):
        return jnp.concatenate([x, x], axis=0)

    def head_sum(x):
        return _dot_exact_rhs(x, head_ones)

    batches = range(n_batch)
    xs, lora = [], []
    for b in batches:
        xs.append(token_shift(rkv_ref[b], prev_ref, b, mu_ref[...]))
        zs = token_shift(zz_ref[b], prevz_ref, b, muz_ref[...])
        act = jnp.where(lane < DECAY_LORA, jnp.tanh(zs), zs)
        lora.append(_dot_x3(act, wl_hi_ref[...], wl_lo_ref[...]))

    chains = [(b, p) for b in batches for p in range(n_blocks)]
    ids = range(len(chains))
    sls = [slice(p * LANES, (p + 1) * LANES) for _, p in chains]
    r = [xs[b][:, p * LANES:(p + 1) * LANES] for b, p in chains]
    k = [xs[b][:, width + p * LANES:width + (p + 1) * LANES] for b, p in chains]
    v = [xs[b][:, 2 * width + p * LANES:2 * width + (p + 1) * LANES] for b, p in chains]
    lw, a = [], []
    for i, (b, p) in enumerate(chains):
        w_log = -_softplus(-(w0[:, sls[i]] + lora[b][:, sls[i]])) - 0.5
        lw.append(-jnp.exp(w_log))
        a.append(_sigmoid(a0[:, sls[i]] + lora[b][:, width + p * LANES:width + (p + 1) * LANES]))

    c = [_dot_cumsum(cum_incl, lw[i]) for i in ids]
    kk_raw = [k[i] * k_k[:, sls[i]] for i in ids]
    kk_ss = [head_sum(kk_raw[i] * kk_raw[i]) for i in ids]
    kmod = [k[i] * (1.0 + (a[i] - 1.0) * k_a[:, sls[i]]) for i in ids]
    bonus_dot = [head_sum(r[i] * kmod[i] * r_k[:, sls[i]]) for i in ids]

    lhs, rhs, xa, xr, vbd, upd, decay_last = [], [], [], [], [], [], []
    for i in ids:
        kk = kk_raw[i] * lax.rsqrt(jnp.maximum(kk_ss[i], 1e-24))
        bvec = kk * a[i]
        c_last = c[i][L - 1:L, :]
        g_in = jnp.exp(c[i])
        g_ex = jnp.exp(c[i] - lw[i])
        g_inv = jnp.exp(-c[i])
        g_rem = jnp.exp(c_last - c[i])
        xa.append(stack_masked(-kk * g_ex))
        xr.append(stack_masked(r[i] * g_in))
        vbd.append(stack_masked(v[i]))
        lhs.append(jnp.concatenate([xa[i], xr[i]], axis=0).astype(BF16))
        rhs.append(jnp.concatenate([stack_plain(bvec * g_inv), stack_plain(kmod[i] * g_inv)], axis=0).astype(BF16))
        upd.append(jnp.concatenate([stack_masked(bvec * g_rem), stack_masked(kmod[i] * g_rem)], axis=0).astype(BF16))
        decay_last.append(jnp.exp(c_last))

    gram = [_dot(lhs[i], rhs[i], NT) for i in ids]
    a_ab = [jnp.where(strict, gram[i][:2 * L, :2 * L], 0.0) for i in ids]
    a_ak = [jnp.where(strict, gram[i][:2 * L, 2 * L:], 0.0) for i in ids]
    a_rbk = [jnp.concatenate([jnp.where(incl, gram[i][2 * L:, :2 * L], 0.0),
                              jnp.where(incl, gram[i][2 * L:, 2 * L:], 0.0)], axis=1).astype(BF16) for i in ids]

    n_sq = L.bit_length() - 1
    nb = [a_ab[i].astype(BF16) for i in ids]
    qmat = [_dot(nb[i], nb[i]) for i in ids]
    w1 = [_dot(a_ak[i].astype(BF16), vbd[i].astype(BF16)) for i in ids]
    pmat = [eye + a_ab[i] for i in ids]
    for step in range(1, n_sq):
        if step < n_sq - 1:
            both = [_dot(jnp.concatenate([pmat[i], qmat[i]], axis=0).astype(BF16), qmat[i].astype(BF16))
                    for i in ids]
            pmat = [pmat[i] + both[i][:2 * L] for i in ids]
            qmat = [both[i][2 * L:] for i in ids]
        else:
            pq = [_dot(pmat[i].astype(BF16), qmat[i].astype(BF16)) for i in ids]
            pmat = [pmat[i] + pq[i] for i in ids]

    aw = [_dot(pmat[i].astype(BF16), jnp.concatenate([xa[i], w1[i]], axis=1).astype(BF16)) for i in ids]
    h_t = [state_ref[i] for i in ids]
    s9 = [_dot(jnp.concatenate([aw[i][:, :LANES], xr[i]], axis=0).astype(BF16), h_t[i].astype(BF16), NT)
          for i in ids]
    uv = [jnp.concatenate([s9[i][:2 * L] + aw[i][:, LANES:], vbd[i]], axis=0).astype(BF16) for i in ids]
    y_bd = [s9[i][2 * L:] + _dot(a_rbk[i], uv[i]) for i in ids]
    for i in ids:
        state_ref[i] = h_t[i] * decay_last[i] + _dot(uv[i], upd[i], TN)

    y = [y_bd[i][:L] + y_bd[i][L:] for i in ids]
    mean = [head_sum(y[i]) * (1.0 / HEAD_DIM) for i in ids]
    dlt = [y[i] - mean[i] for i in ids]
    var = [head_sum(dlt[i] * dlt[i]) * (1.0 / HEAD_DIM) for i in ids]
    for i, (b, p) in enumerate(chains):
        yn = dlt[i] * lax.rsqrt(var[i] + GN_EPS) * ln_w[:, sls[i]] + ln_b[:, sls[i]]
        gate = g_ref[b, :, sls[i]]
        y_ref[b, :, sls[i]] = ((yn + bonus_dot[i] * v[i]) * _silu(gate)).astype(y_ref.dtype)


def _dot_cumsum(cum_incl, x):
    hi, mid, lo = _split3(x)
    return _dot(cum_incl, hi) + _dot(cum_incl, mid) + _dot(cum_incl, lo)


def _rwkv(rw, zz, mu, w0, w2, a0, a2, k_k, k_a, r_k, ln_w, ln_b):
    b, s, w4 = rw.shape
    width = w4 // 4
    n_blocks = width // LANES
    L = RWKV_CHUNK
    mu_rkv = mu[:3 * width].reshape(1, 3 * width)
    mu_z = mu[3 * width:].reshape(1, DECAY_LORA + ICLR_LORA)
    w_lora = jnp.zeros((DECAY_LORA + ICLR_LORA, 2 * width), F32)
    w_lora = w_lora.at[:DECAY_LORA, :width].set(w2).at[DECAY_LORA:, width:].set(a2)
    wl_hi = w_lora.astype(BF16)
    wl_lo = (w_lora - wl_hi.astype(F32)).astype(BF16)
    vec = jnp.stack([k_k, k_a, r_k.reshape(width), ln_w, ln_b, w0, a0, jnp.zeros_like(w0)], axis=0)
    return pl.pallas_call(
        functools.partial(_rwkv_kernel, n_blocks=n_blocks),
        grid=(s // L,),
        in_specs=[
            pl.BlockSpec((b, L, 3 * width), lambda j: (0, j, 0)),
            pl.BlockSpec((b, L, LANES), lambda j: (0, j, 0)),
            pl.BlockSpec((b, L, width), lambda j: (0, j, 3)),
            pl.BlockSpec((1, 3 * width), lambda j: (0, 0)),
            pl.BlockSpec((1, LANES), lambda j: (0, 0)),
            pl.BlockSpec((LANES, 2 * width), lambda j: (0, 0)),
            pl.BlockSpec((LANES, 2 * width), lambda j: (0, 0)),
            pl.BlockSpec((8, width), lambda j: (0, 0)),
        ],
        out_specs=pl.BlockSpec((b, L, width), lambda j: (0, j, 0)),
        out_shape=jax.ShapeDtypeStruct((b, s, width), BF16),
        scratch_shapes=[
            pltpu.VMEM((b * n_blocks, LANES, LANES), F32),
            pltpu.VMEM((b, 8, 3 * width), F32),
            pltpu.VMEM((b, 8, LANES), F32),
        ],
        compiler_params=pltpu.CompilerParams(
            dimension_semantics=("arbitrary",), vmem_limit_bytes=VMEM_LIMIT_BYTES),
        name="rwkv7",
    )(rw, zz, rw, mu_rkv, mu_z, wl_hi, wl_lo, vec)


def _sb_kernel(q_ref, k_ref, v_ref, sg_ref, o_ref, acc_ref, carry_ref, *, n_blocks):
    T = SB_BLOCK
    qi = pl.program_id(1)
    chains = range(n_blocks)
    cols = [slice(p * LANES, (p + 1) * LANES) for p in chains]
    lane = lax.broadcasted_iota(jnp.int32, (T, LANES), 1)
    head0 = lane < HEAD_DIM
    scale = HEAD_DIM ** -0.5
    qs = []
    for p in chains:
        q = q_ref[0, :, cols[p]]
        zero = jnp.zeros_like(q)
        qs.append(jnp.concatenate([jnp.where(head0, q, zero), jnp.where(head0, zero, q)], axis=0) * scale)

    sr = lax.broadcasted_iota(jnp.int32, (T, T), 0)
    sc = lax.broadcasted_iota(jnp.int32, (T, T), 1)
    tri_ones = jnp.concatenate([jnp.where(sr > sc, 1.0, 0.0), jnp.ones((T, T), F32)], axis=1).astype(BF16)
    mr = lax.broadcasted_iota(jnp.int32, (2 * T, T), 0)
    mc = lax.broadcasted_iota(jnp.int32, (2 * T, T), 1)
    causal = mc < jnp.where(mr >= T, mr - T, mr)

    acc_ref[...] = jnp.zeros_like(acc_ref)
    carry_ref[...] = jnp.zeros_like(carry_ref)

    def tiles(j, diag):
        start = pl.multiple_of(j * T, T)
        z = [_dot(qs[p], k_ref[0, pl.ds(start, T), cols[p]], NT) for p in chains]
        soft = [jnp.log(1.0 + jnp.exp(-jnp.abs(z[p]))) for p in chains]
        log_1m = [-(jnp.maximum(z[p], 0.0) + soft[p]) for p in chains]
        if diag:
            log_1m = [jnp.where(causal, log_1m[p], 0.0) for p in chains]
        parts = [_split2(log_1m[p]) for p in chains]
        sums = [_dot(parts[p][0], tri_ones) + _dot(parts[p][1], tri_ones) for p in chains]
        for p in chains:
            carry = carry_ref[p]
            log_b = jnp.minimum(z[p], 0.0) - soft[p]
            attn = jnp.exp(log_b + sums[p][:, :T] + carry)
            if diag:
                attn = jnp.where(causal, attn, 0.0)
            acc_ref[p] += _dot(attn.astype(BF16), v_ref[0, pl.ds(start, T), cols[p]])
            carry_ref[p] = carry + sums[p][:, T:]

    def live():
        top = carry_ref[0]
        for p in chains[1:]:
            top = jnp.maximum(top, carry_ref[p])
        return (jnp.max(top) > -SB_F32_EXP_UNDERFLOW).astype(jnp.int32)

    tiles(qi, True)

    def cond(state):
        it, alive = state
        return jnp.logical_and(it < qi, alive > 0)

    def body(state):
        it, _ = state
        tiles(qi - 1 - it, False)
        return it + 1, live()

    lax.while_loop(cond, body, (jnp.int32(0), live()))

    head_ones = _head_ones()
    for p in chains:
        acc = acc_ref[p]
        o = jnp.where(head0, acc[:T], acc[T:])
        ms = _dot_exact_rhs(o * o, head_ones) * (1.0 / HEAD_DIM)
        o_ref[0, :, cols[p]] = o * lax.rsqrt(ms + NORM_EPS) * sg_ref[:, cols[p]]


def _stick_breaking(qkv, sb_g):
    b, s, w3 = qkv.shape
    width = w3 // 3
    n_blocks = width // LANES
    T = SB_BLOCK
    return pl.pallas_call(
        functools.partial(_sb_kernel, n_blocks=n_blocks),
        grid=(b, s // T),
        in_specs=[
            pl.BlockSpec((1, T, width), lambda i, j: (i, j, 0)),
            pl.BlockSpec((1, s, width), lambda i, j: (i, 0, 1)),
            pl.BlockSpec((1, s, width), lambda i, j: (i, 0, 2)),
            pl.BlockSpec((1, width), lambda i, j: (0, 0)),
        ],
        out_specs=pl.BlockSpec((1, T, width), lambda i, j: (i, j, 0)),
        out_shape=jax.ShapeDtypeStruct((b, s, width), F32),
        scratch_shapes=[
            pltpu.VMEM((n_blocks, 2 * T, LANES), F32),
            pltpu.VMEM((n_blocks, 2 * T, T), F32),
        ],
        compiler_params=pltpu.CompilerParams(
            dimension_semantics=("parallel", "arbitrary"), vmem_limit_bytes=VMEM_LIMIT_BYTES),
        name="stick_breaking",
    )(qkv, qkv, qkv, sb_g.reshape(1, width))


def _out_proj_kernel(yr_ref, os_ref, sg_ref, w_ref, x_ref, gate_ref, fg_ref, o_ref, perm_ref, *, final_norm):
    half = yr_ref.shape[-1]
    n_qb, sub = os_ref.shape[1], os_ref.shape[2]
    o_tok = os_ref[0].reshape(n_qb * sub, half)
    cols = []
    for cb in range(half // LANES):
        perm_ref[cb] = o_tok[:, cb * LANES:(cb + 1) * LANES]
        cols.append(jnp.concatenate(
            [perm_ref[cb, pl.ds(ii, n_qb, stride=sub), :] for ii in range(sub)], axis=0))
    o_rows = jnp.concatenate(cols, axis=1)
    y_sb = (o_rows * _silu(sg_ref[0])).astype(BF16)
    y = _dot(yr_ref[0], w_ref[:half, :]) + _dot(y_sb, w_ref[half:, :])
    x = x_ref[0] + gate_ref[0] * y
    if final_norm:
        ms = jnp.mean(x * x, axis=-1, keepdims=True)
        x = x * lax.rsqrt(ms + NORM_EPS) * fg_ref[...]
    o_ref[0] = x


def _out_proj(y_rw, o_sb, sb_gate, w_bf16, x, gate, final_g, final_norm):
    b, s, d = x.shape
    half = y_rw.shape[-1]
    n_qb = s // SB_BLOCK
    sub = 8
    tm = sub * n_qb
    o_sb4 = o_sb.reshape(b, n_qb, SB_BLOCK, half)
    return pl.pallas_call(
        functools.partial(_out_proj_kernel, final_norm=final_norm),
        grid=(b, s // tm),
        in_specs=[
            pl.BlockSpec((1, tm, half), lambda i, j: (i, j, 0)),
            pl.BlockSpec((1, n_qb, sub, half), lambda i, j: (i, 0, j, 0)),
            pl.BlockSpec((1, tm, half), lambda i, j: (i, j, 0)),
            pl.BlockSpec((2 * half, d), lambda i, j: (0, 0)),
            pl.BlockSpec((1, tm, d), lambda i, j: (i, j, 0)),
            pl.BlockSpec((1, 1, d), lambda i, j: (i, 0, 0)),
            pl.BlockSpec((1, d), lambda i, j: (0, 0)),
        ],
        out_specs=pl.BlockSpec((1, tm, d), lambda i, j: (i, j, 0)),
        out_shape=jax.ShapeDtypeStruct((b, s, d), F32),
        scratch_shapes=[pltpu.VMEM((half // LANES, tm, LANES), F32)],
        compiler_params=pltpu.CompilerParams(
            dimension_semantics=("parallel", "parallel"), vmem_limit_bytes=VMEM_LIMIT_BYTES),
        name="out_proj",
    )(y_rw, o_sb4, sb_gate, w_bf16, x, gate.reshape(b, 1, d), final_g.reshape(1, d))


def _row_tile(s):
    for tm in (512, 256, 128, 64, 32, 16, 8):
        if s % tm == 0:
            return tm
    raise ValueError(f"sequence length {s} must be a multiple of 8")


def kernel(x, c, norm_g, ada_w, ada_b, w_in, w_out, tshift_mu, decay_w0, decay_w2, iclr_a0, iclr_a2,
           k_k, k_a, r_k, rwkv_ln_w, rwkv_ln_b, sb_norm_g, final_g):
    depth = norm_g.shape[0]
    b, s, d = x.shape
    assert s % SB_BLOCK == 0 and s % RWKV_CHUNK == 0
    tm = _row_tile(s)
    mod = _ada_mod(c, ada_w, ada_b)
    w_in_b = w_in.astype(BF16)
    w_out_b = w_out.astype(BF16)
    for l in range(depth):
        shift, scale, gate = mod[l, :, :d], mod[l, :, d:2 * d], mod[l, :, 2 * d:]
        rw, zz, sb_qkv, sb_gate = _in_proj(x, norm_g[l], scale, shift, w_in_b[l], tm)
        y_rw = _rwkv(rw, zz, tshift_mu[l], decay_w0[l], decay_w2[l], iclr_a0[l], iclr_a2[l],
                     k_k[l], k_a[l], r_k[l], rwkv_ln_w[l], rwkv_ln_b[l])
        o_sb = _stick_breaking(sb_qkv, sb_norm_g[l])
        x = _out_proj(y_rw, o_sb, sb_gate, w_out_b[l], x, gate, final_g, l == depth - 1)
    return x
```

```python
import functools

import jax
import jax.numpy as jnp
from jax import lax
from jax.experimental import pallas as pl
from jax.experimental.pallas import tpu as pltpu

F32 = jnp.float32
BF16 = jnp.bfloat16

HEAD_DIM = 64
LANES = 128
HEADS_PER_BLOCK = LANES // HEAD_DIM
RWKV_CHUNK = 64
SB_BLOCK = 128
SB_F32_EXP_UNDERFLOW = 106.0
DECAY_LORA = 64
ICLR_LORA = 64
NORM_EPS = 1e-6
GN_EPS = 64e-5
VMEM_LIMIT_BYTES = 56 * 1024 * 1024

NN = (((1,), (0,)), ((), ()))
NT = (((1,), (1,)), ((), ()))
TN = (((0,), (0,)), ((), ()))


def _dot(a, b, dims=NN):
    return lax.dot_general(a, b, dims, preferred_element_type=F32)


def _split2(x):
    hi = x.astype(BF16)
    lo = (x - hi.astype(F32)).astype(BF16)
    return hi, lo


def _split3(x):
    hi = x.astype(BF16)
    r1 = x - hi.astype(F32)
    mid = r1.astype(BF16)
    lo = (r1 - mid.astype(F32)).astype(BF16)
    return hi, mid, lo


def _dot_exact_rhs(x, rhs_bf16, dims=NN):
    hi, mid, lo = _split3(x)
    return _dot(hi, rhs_bf16, dims) + _dot(mid, rhs_bf16, dims) + _dot(lo, rhs_bf16, dims)


def _dot_x3(a, b_hi, b_lo):
    a_hi, a_lo = _split2(a)
    return _dot(a_hi, b_hi) + _dot(a_lo, b_hi) + _dot(a_hi, b_lo)


def _softplus(u):
    return jnp.maximum(u, 0.0) + jnp.log(1.0 + jnp.exp(-jnp.abs(u)))


def _sigmoid(u):
    return 1.0 / (1.0 + jnp.exp(-u))


def _silu(u):
    return u * _sigmoid(u)


def _head_ones():
    r = lax.broadcasted_iota(jnp.int32, (LANES, LANES), 0)
    c = lax.broadcasted_iota(jnp.int32, (LANES, LANES), 1)
    return jnp.where((r >= HEAD_DIM) == (c >= HEAD_DIM), 1.0, 0.0).astype(BF16)


def _ada_kernel(c_ref, w_ref, b_ref, o_ref):
    c_act = _silu(c_ref[...])
    w_hi, w_lo = _split2(w_ref[0])
    o_ref[0] = _dot_x3(c_act, w_hi, w_lo) + b_ref[0]


def _ada_mod(c, ada_w, ada_b):
    depth, d, d3 = ada_w.shape
    b = c.shape[0]
    rows = 8
    c8 = jnp.zeros((rows, d), F32).at[:b].set(c)
    out = pl.pallas_call(
        _ada_kernel,
        grid=(depth, d3 // d),
        in_specs=[
            pl.BlockSpec((rows, d), lambda l, j: (0, 0)),
            pl.BlockSpec((1, d, d), lambda l, j: (l, 0, j)),
            pl.BlockSpec((1, 1, d), lambda l, j: (l, 0, j)),
        ],
        out_specs=pl.BlockSpec((1, rows, d), lambda l, j: (l, 0, j)),
        out_shape=jax.ShapeDtypeStruct((depth, rows, d3), F32),
        name="ada_mod",
    )(c8, ada_w, ada_b.reshape(depth, 1, d3))
    return out[:, :b]


def _in_proj_kernel(x_ref, g_ref, sc_ref, sh_ref, w_ref, rw_ref, zz_ref, sbq_ref, sbg_ref, *, widths):
    rw_w, zz_w, sbq_w, sbg_w = widths
    x = x_ref[0]
    ms = jnp.mean(x * x, axis=-1, keepdims=True)
    h = x * lax.rsqrt(ms + NORM_EPS) * g_ref[...]
    h = h * (1.0 + sc_ref[0]) + sh_ref[0]
    hb = h.astype(BF16)

    def emit(out_ref, out_col, w_col, width):
        step = 512
        for c0 in range(0, width, step):
            cw = min(step, width - c0)
            res = _dot(hb, w_ref[:, w_col + c0:w_col + c0 + cw])
            out_ref[0, :, out_col + c0:out_col + c0 + cw] = res.astype(out_ref.dtype)

    shift_rkv = rw_w - sbg_w
    emit(rw_ref, 0, 0, shift_rkv)
    emit(zz_ref, 0, shift_rkv, zz_w)
    emit(rw_ref, shift_rkv, shift_rkv + zz_w, sbg_w)
    sb0 = rw_w + zz_w
    emit(sbq_ref, 0, sb0, sbq_w)
    emit(sbg_ref, 0, sb0 + sbq_w, sbg_w)


def _in_proj(x, g, scale, shift, w_bf16, tm):
    b, s, d = x.shape
    width = d // 2
    widths = (4 * width, DECAY_LORA + ICLR_LORA, 3 * width, width)
    n_cols = w_bf16.shape[1]
    assert sum(widths) == n_cols
    return pl.pallas_call(
        functools.partial(_in_proj_kernel, widths=widths),
        grid=(b, s // tm),
        in_specs=[
            pl.BlockSpec((1, tm, d), lambda i, j: (i, j, 0)),
            pl.BlockSpec((1, d), lambda i, j: (0, 0)),
            pl.BlockSpec((1, 1, d), lambda i, j: (i, 0, 0)),
            pl.BlockSpec((1, 1, d), lambda i, j: (i, 0, 0)),
            pl.BlockSpec((d, n_cols), lambda i, j: (0, 0)),
        ],
        out_specs=[
            pl.BlockSpec((1, tm, widths[0]), lambda i, j: (i, j, 0)),
            pl.BlockSpec((1, tm, widths[1]), lambda i, j: (i, j, 0)),
            pl.BlockSpec((1, tm, widths[2]), lambda i, j: (i, j, 0)),
            pl.BlockSpec((1, tm, widths[3]), lambda i, j: (i, j, 0)),
        ],
        out_shape=[
            jax.ShapeDtypeStruct((b, s, widths[0]), F32),
            jax.ShapeDtypeStruct((b, s, widths[1]), F32),
            jax.ShapeDtypeStruct((b, s, widths[2]), BF16),
            jax.ShapeDtypeStruct((b, s, widths[3]), F32),
        ],
        compiler_params=pltpu.CompilerParams(
            dimension_semantics=("parallel", "parallel"), vmem_limit_bytes=VMEM_LIMIT_BYTES),
        name="in_proj",
    )(x, g.reshape(1, d), scale.reshape(b, 1, d), shift.reshape(b, 1, d), w_bf16)


def _rwkv_kernel(rkv_ref, zz_ref, g_ref, mu_ref, muz_ref, wl_ref, vec_ref, y_ref,
                 state_ref, prev_ref, prevz_ref, *, n_blocks):
    L = RWKV_CHUNK
    n_batch = rkv_ref.shape[0]
    width = n_blocks * LANES
    t = pl.program_id(0)

    @pl.when(t == 0)
    def _():
        state_ref[...] = jnp.zeros_like(state_ref)
        prev_ref[...] = jnp.zeros_like(prev_ref)
        prevz_ref[...] = jnp.zeros_like(prevz_ref)

    def token_shift(x, carry_ref, b, mu):
        row = lax.broadcasted_iota(jnp.int32, x.shape, 0)
        prev = jnp.where(row == 0, carry_ref[b, 7:8, :], pltpu.roll(x, 1, 0))
        carry_ref[b] = x[L - 8:L]
        return x + (prev - x) * mu

    k_k = vec_ref[0:1, :]
    k_a = vec_ref[1:2, :]
    r_k = vec_ref[2:3, :]
    ln_w = vec_ref[3:4, :]
    ln_b = vec_ref[4:5, :]
    w0 = vec_ref[5:6, :]
    a0 = vec_ref[6:7, :]

    lane = lax.broadcasted_iota(jnp.int32, (L, LANES), 1)
    head0 = lane < HEAD_DIM
    tr = lax.broadcasted_iota(jnp.int32, (L, 3 * L), 0)
    tc = lax.broadcasted_iota(jnp.int32, (L, 3 * L), 1)
    tc = jnp.where(tc >= 2 * L, tc - 2 * L, jnp.where(tc >= L, tc - L, tc))
    cum3 = jnp.where(tc <= tr, 1.0, 0.0).astype(BF16)
    pr = lax.broadcasted_iota(jnp.int32, (L, 2 * L), 0)
    pc = lax.broadcasted_iota(jnp.int32, (L, 2 * L), 1)
    pc = jnp.where(pc >= L, pc - L, pc)
    strict = pc < pr
    incl = pc <= pr
    eye2 = jnp.where(pc == pr, 1.0, 0.0)
    br = lax.broadcasted_iota(jnp.int32, (LANES, LANES), 0)
    bc = lax.broadcasted_iota(jnp.int32, (LANES, LANES), 1)
    same_head = (br >= HEAD_DIM) == (bc >= HEAD_DIM)

    def stacked(x):
        xb = x.astype(BF16)
        zero = jnp.zeros_like(xb)
        return jnp.concatenate([jnp.where(head0, xb, zero), jnp.where(head0, zero, xb)], axis=0)

    def head_sum(x):
        s0 = jnp.sum(jnp.where(head0, x, 0.0), axis=1, keepdims=True)
        s1 = jnp.sum(jnp.where(head0, 0.0, x), axis=1, keepdims=True)
        return jnp.where(head0, s0, s1)

    batches = range(n_batch)
    xs, lora = [], []
    for b in batches:
        xs.append(token_shift(rkv_ref[b], prev_ref, b, mu_ref[...]))
        zs = token_shift(zz_ref[b], prevz_ref, b, muz_ref[...])
        act = jnp.where(lane < DECAY_LORA, jnp.tanh(zs), zs)
        a_hi, a_lo = _split2(act)
        lora.append(_dot(jnp.concatenate([a_hi, a_lo, a_hi], axis=1), wl_ref[...]))

    chains = [(b, p) for b in batches for p in range(n_blocks)]
    ids = range(len(chains))
    sls = [slice(p * LANES, (p + 1) * LANES) for _, p in chains]
    r = [xs[b][:, p * LANES:(p + 1) * LANES] for b, p in chains]
    k = [xs[b][:, width + p * LANES:width + (p + 1) * LANES] for b, p in chains]
    v = [xs[b][:, 2 * width + p * LANES:2 * width + (p + 1) * LANES] for b, p in chains]
    lw, a = [], []
    for i, (b, p) in enumerate(chains):
        w_log = -_softplus(-(w0[:, sls[i]] + lora[b][:, sls[i]])) - 0.5
        lw.append(-jnp.exp(w_log))
        a.append(_sigmoid(a0[:, sls[i]] + lora[b][:, width + p * LANES:width + (p + 1) * LANES]))

    c = [_dot(cum3, jnp.concatenate(_split3(lw[i]), axis=0)) for i in ids]
    kk_raw = [k[i] * k_k[:, sls[i]] for i in ids]
    kk_ss = [head_sum(kk_raw[i] * kk_raw[i]) for i in ids]
    kmod = [k[i] * (1.0 + (a[i] - 1.0) * k_a[:, sls[i]]) for i in ids]
    bonus_dot = [head_sum(r[i] * kmod[i] * r_k[:, sls[i]]) for i in ids]

    lhs, rhs, xa_s, xr_b, v_s, upd, decay_last = [], [], [], [], [], [], []
    for i in ids:
        kk = kk_raw[i] * lax.rsqrt(jnp.maximum(kk_ss[i], 1e-24))
        bvec = kk * a[i]
        c_last = c[i][L - 1:L, :]
        g_in = jnp.exp(c[i])
        g_ex = jnp.exp(c[i] - lw[i])
        g_inv = jnp.exp(-c[i])
        g_rem = jnp.exp(c_last - c[i])
        xa = -kk * g_ex
        xr_b.append((r[i] * g_in).astype(BF16))
        xa_s.append(stacked(xa))
        v_s.append(stacked(v[i]))
        lhs.append(jnp.concatenate([xa.astype(BF16), xr_b[i]], axis=0))
        rhs.append(jnp.concatenate([stacked(bvec * g_inv), stacked(kmod[i] * g_inv)], axis=0))
        upd.append(jnp.concatenate([bvec * g_rem, kmod[i] * g_rem], axis=0).astype(BF16))
        decay_last.append(jnp.exp(c_last))

    gram = [_dot(lhs[i], rhs[i], NT) for i in ids]
    n_p = [jnp.where(strict, gram[i][:L, :2 * L], 0.0) for i in ids]
    a_ak = [jnp.where(strict, gram[i][:L, 2 * L:], 0.0).astype(BF16) for i in ids]
    a_rbk = [jnp.concatenate([jnp.where(incl, gram[i][L:, :2 * L], 0.0),
                              jnp.where(incl, gram[i][L:, 2 * L:], 0.0)], axis=1).astype(BF16) for i in ids]

    n_sq = L.bit_length() - 1
    qmat = [_dot(n_p[i].astype(BF16), stacked(n_p[i])) for i in ids]
    w1 = [_dot(a_ak[i], v_s[i]) for i in ids]
    pmat = [eye2 + n_p[i] for i in ids]
    for step in range(1, n_sq):
        q_s = [stacked(qmat[i]) for i in ids]
        if step < n_sq - 1:
            both = [_dot(jnp.concatenate([pmat[i], qmat[i]], axis=0).astype(BF16), q_s[i]) for i in ids]
            pmat = [pmat[i] + both[i][:L] for i in ids]
            qmat = [both[i][L:] for i in ids]
        else:
            pq = [_dot(pmat[i].astype(BF16), q_s[i]) for i in ids]
            pmat = [pmat[i] + pq[i] for i in ids]

    aw = [_dot(pmat[i].astype(BF16), jnp.concatenate([xa_s[i], stacked(w1[i])], axis=1)) for i in ids]
    h_t = [state_ref[i] for i in ids]
    s9 = [_dot(jnp.concatenate([aw[i][:, :LANES].astype(BF16), xr_b[i]], axis=0), h_t[i].astype(BF16), NT)
          for i in ids]
    u = [s9[i][:L] + aw[i][:, LANES:] for i in ids]
    y = [s9[i][L:] + _dot(a_rbk[i], jnp.concatenate([stacked(u[i]), v_s[i]], axis=0)) for i in ids]
    for i in ids:
        uv = jnp.concatenate([u[i], v[i]], axis=0).astype(BF16)
        state_ref[i] = h_t[i] * decay_last[i] + jnp.where(same_head, _dot(uv, upd[i], TN), 0.0)

    mean = [head_sum(y[i]) * (1.0 / HEAD_DIM) for i in ids]
    dlt = [y[i] - mean[i] for i in ids]
    var = [head_sum(dlt[i] * dlt[i]) * (1.0 / HEAD_DIM) for i in ids]
    for i, (b, p) in enumerate(chains):
        yn = dlt[i] * lax.rsqrt(var[i] + GN_EPS) * ln_w[:, sls[i]] + ln_b[:, sls[i]]
        gate = g_ref[b, :, sls[i]]
        y_ref[b, :, sls[i]] = ((yn + bonus_dot[i] * v[i]) * _silu(gate)).astype(y_ref.dtype)


def _rwkv(rw, zz, mu, w0, w2, a0, a2, k_k, k_a, r_k, ln_w, ln_b):
    b, s, w4 = rw.shape
    width = w4 // 4
    n_blocks = width // LANES
    L = RWKV_CHUNK
    mu_rkv = mu[:3 * width].reshape(1, 3 * width)
    mu_z = mu[3 * width:].reshape(1, DECAY_LORA + ICLR_LORA)
    w_lora = jnp.zeros((DECAY_LORA + ICLR_LORA, 2 * width), F32)
    w_lora = w_lora.at[:DECAY_LORA, :width].set(w2).at[DECAY_LORA:, width:].set(a2)
    wl_hi = w_lora.astype(BF16)
    wl_lo = (w_lora - wl_hi.astype(F32)).astype(BF16)
    wl_cat = jnp.concatenate([wl_hi, wl_hi, wl_lo], axis=0)
    vec = jnp.stack([k_k, k_a, r_k.reshape(width), ln_w, ln_b, w0, a0, jnp.zeros_like(w0)], axis=0)
    return pl.pallas_call(
        functools.partial(_rwkv_kernel, n_blocks=n_blocks),
        grid=(s // L,),
        in_specs=[
            pl.BlockSpec((b, L, 3 * width), lambda j: (0, j, 0)),
            pl.BlockSpec((b, L, LANES), lambda j: (0, j, 0)),
            pl.BlockSpec((b, L, width), lambda j: (0, j, 3)),
            pl.BlockSpec((1, 3 * width), lambda j: (0, 0)),
            pl.BlockSpec((1, LANES), lambda j: (0, 0)),
            pl.BlockSpec((3 * LANES, 2 * width), lambda j: (0, 0)),
            pl.BlockSpec((8, width), lambda j: (0, 0)),
        ],
        out_specs=pl.BlockSpec((b, L, width), lambda j: (0, j, 0)),
        out_shape=jax.ShapeDtypeStruct((b, s, width), BF16),
        scratch_shapes=[
            pltpu.VMEM((b * n_blocks, LANES, LANES), F32),
            pltpu.VMEM((b, 8, 3 * width), F32),
            pltpu.VMEM((b, 8, LANES), F32),
        ],
        compiler_params=pltpu.CompilerParams(
            dimension_semantics=("arbitrary",), vmem_limit_bytes=VMEM_LIMIT_BYTES),
        name="rwkv7",
    )(rw, zz, rw, mu_rkv, mu_z, wl_cat, vec)


def _sb_kernel(q_ref, k_ref, v_ref, sg_ref, o_ref, acc_ref, carry_ref, *, n_blocks):
    T = SB_BLOCK
    qi = pl.program_id(0)
    chains = [(b, slice(p * LANES, (p + 1) * LANES)) for b in range(q_ref.shape[0]) for p in range(n_blocks)]
    ids = range(len(chains))
    lane = lax.broadcasted_iota(jnp.int32, (T, LANES), 1)
    head0 = lane < HEAD_DIM
    scale = HEAD_DIM ** -0.5
    qs = []
    for b, cols in chains:
        q = q_ref[b, :, cols]
        zero = jnp.zeros_like(q)
        qs.append(jnp.concatenate([jnp.where(head0, q, zero), jnp.where(head0, zero, q)], axis=0) * scale)

    sr = lax.broadcasted_iota(jnp.int32, (T, T), 0)
    sc = lax.broadcasted_iota(jnp.int32, (T, T), 1)
    tri_ones = jnp.concatenate([jnp.where(sr > sc, 1.0, 0.0), jnp.ones((T, T), F32)], axis=1).astype(BF16)
    mr = lax.broadcasted_iota(jnp.int32, (2 * T, T), 0)
    mc = lax.broadcasted_iota(jnp.int32, (2 * T, T), 1)
    causal = mc < jnp.where(mr >= T, mr - T, mr)

    acc_ref[...] = jnp.zeros_like(acc_ref)
    carry_ref[...] = jnp.zeros_like(carry_ref)

    def tiles(j, diag):
        start = pl.multiple_of(j * T, T)
        z = [_dot(qs[i], k_ref[b, pl.ds(start, T), cols], NT) for i, (b, cols) in enumerate(chains)]
        soft = [jnp.log(1.0 + jnp.exp(-jnp.abs(z[i]))) for i in ids]
        log_1m = [-(jnp.maximum(z[i], 0.0) + soft[i]) for i in ids]
        if diag:
            log_1m = [jnp.where(causal, log_1m[i], 0.0) for i in ids]
        parts = [_split2(log_1m[i]) for i in ids]
        sums = [_dot(parts[i][0], tri_ones) + _dot(parts[i][1], tri_ones) for i in ids]
        for i, (b, cols) in enumerate(chains):
            carry = carry_ref[i]
            log_b = jnp.minimum(z[i], 0.0) - soft[i]
            attn = jnp.exp(log_b + sums[i][:, :T] + carry)
            if diag:
                attn = jnp.where(causal, attn, 0.0)
            acc_ref[i] += _dot(attn.astype(BF16), v_ref[b, pl.ds(start, T), cols])
            carry_ref[i] = carry + sums[i][:, T:]

    def live():
        top = carry_ref[0]
        for i in ids[1:]:
            top = jnp.maximum(top, carry_ref[i])
        return (jnp.max(top) > -SB_F32_EXP_UNDERFLOW).astype(jnp.int32)

    tiles(qi, True)

    def cond(state):
        it, alive = state
        return jnp.logical_and(it < qi, alive > 0)

    def body(state):
        it, _ = state
        tiles(qi - 1 - it, False)
        return it + 1, live()

    lax.while_loop(cond, body, (jnp.int32(0), live()))

    head_ones = _head_ones()
    for i, (b, cols) in enumerate(chains):
        acc = acc_ref[i]
        o = jnp.where(head0, acc[:T], acc[T:])
        ms = _dot_exact_rhs(o * o, head_ones) * (1.0 / HEAD_DIM)
        o_ref[b, :, cols] = o * lax.rsqrt(ms + NORM_EPS) * sg_ref[:, cols]


def _stick_breaking(qkv, sb_g):
    b, s, w3 = qkv.shape
    width = w3 // 3
    n_blocks = width // LANES
    T = SB_BLOCK
    resident = pl.Buffered(1)
    return pl.pallas_call(
        functools.partial(_sb_kernel, n_blocks=n_blocks),
        grid=(s // T,),
        in_specs=[
            pl.BlockSpec((b, T, width), lambda j: (0, j, 0)),
            pl.BlockSpec((b, s, width), lambda j: (0, 0, 1), pipeline_mode=resident),
            pl.BlockSpec((b, s, width), lambda j: (0, 0, 2), pipeline_mode=resident),
            pl.BlockSpec((1, width), lambda j: (0, 0)),
        ],
        out_specs=pl.BlockSpec((b, T, width), lambda j: (0, j, 0)),
        out_shape=jax.ShapeDtypeStruct((b, s, width), F32),
        scratch_shapes=[
            pltpu.VMEM((b * n_blocks, 2 * T, LANES), F32),
            pltpu.VMEM((b * n_blocks, 2 * T, T), F32),
        ],
        compiler_params=pltpu.CompilerParams(
            dimension_semantics=("arbitrary",), vmem_limit_bytes=VMEM_LIMIT_BYTES),
        name="stick_breaking",
    )(qkv, qkv, qkv, sb_g.reshape(1, width))


def _out_proj_kernel(yr_ref, os_ref, sg_ref, w_ref, x_ref, gate_ref, fg_ref, o_ref, perm_ref, *, final_norm):
    half = yr_ref.shape[-1]
    n_qb, sub = os_ref.shape[1], os_ref.shape[2]
    o_tok = os_ref[0].reshape(n_qb * sub, half)
    cols = []
    for cb in range(half // LANES):
        perm_ref[cb] = o_tok[:, cb * LANES:(cb + 1) * LANES]
        cols.append(jnp.concatenate(
            [perm_ref[cb, pl.ds(ii, n_qb, stride=sub), :] for ii in range(sub)], axis=0))
    o_rows = jnp.concatenate(cols, axis=1)
    y_sb = (o_rows * _silu(sg_ref[0])).astype(BF16)
    y = _dot(yr_ref[0], w_ref[:half, :]) + _dot(y_sb, w_ref[half:, :])
    x = x_ref[0] + gate_ref[0] * y
    if final_norm:
        ms = jnp.mean(x * x, axis=-1, keepdims=True)
        x = x * lax.rsqrt(ms + NORM_EPS) * fg_ref[...]
    o_ref[0] = x


def _out_proj(y_rw, o_sb, sb_gate, w_bf16, x, gate, final_g, final_norm):
    b, s, d = x.shape
    half = y_rw.shape[-1]
    n_qb = s // SB_BLOCK
    sub = 8
    tm = sub * n_qb
    o_sb4 = o_sb.reshape(b, n_qb, SB_BLOCK, half)
    return pl.pallas_call(
        functools.partial(_out_proj_kernel, final_norm=final_norm),
        grid=(b, s // tm),
        in_specs=[
            pl.BlockSpec((1, tm, half), lambda i, j: (i, j, 0)),
            pl.BlockSpec((1, n_qb, sub, half), lambda i, j: (i, 0, j, 0)),
            pl.BlockSpec((1, tm, half), lambda i, j: (i, j, 0)),
            pl.BlockSpec((2 * half, d), lambda i, j: (0, 0)),
            pl.BlockSpec((1, tm, d), lambda i, j: (i, j, 0)),
            pl.BlockSpec((1, 1, d), lambda i, j: (i, 0, 0)),
            pl.BlockSpec((1, d), lambda i, j: (0, 0)),
        ],
        out_specs=pl.BlockSpec((1, tm, d), lambda i, j: (i, j, 0)),
        out_shape=jax.ShapeDtypeStruct((b, s, d), F32),
        scratch_shapes=[pltpu.VMEM((half // LANES, tm, LANES), F32)],
        compiler_params=pltpu.CompilerParams(
            dimension_semantics=("parallel", "parallel"), vmem_limit_bytes=VMEM_LIMIT_BYTES),
        name="out_proj",
    )(y_rw, o_sb4, sb_gate, w_bf16, x, gate.reshape(b, 1, d), final_g.reshape(1, d))


def _row_tile(s):
    for tm in (512, 256, 128, 64, 32, 16, 8):
        if s % tm == 0:
            return tm
    raise ValueError(f"sequence length {s} must be a multiple of 8")


def kernel(x, c, norm_g, ada_w, ada_b, w_in, w_out, tshift_mu, decay_w0, decay_w2, iclr_a0, iclr_a2,
           k_k, k_a, r_k, rwkv_ln_w, rwkv_ln_b, sb_norm_g, final_g):
    depth = norm_g.shape[0]
    b, s, d = x.shape
    assert s % SB_BLOCK == 0 and s % RWKV_CHUNK == 0
    tm = _row_tile(s)
    mod = _ada_mod(c, ada_w, ada_b)
    w_in_b = w_in.astype(BF16)
    w_out_b = w_out.astype(BF16)
    for l in range(depth):
        shift, scale, gate = mod[l, :, :d], mod[l, :, d:2 * d], mod[l, :, 2 * d:]
        rw, zz, sb_qkv, sb_gate = _in_proj(x, norm_g[l], scale, shift, w_in_b[l], tm)
        y_rw = _rwkv(rw, zz, tshift_mu[l], decay_w0[l], decay_w2[l], iclr_a0[l], iclr_a2[l],
                     k_k[l], k_a[l], r_k[l], rwkv_ln_w[l], rwkv_ln_b[l])
        o_sb = _stick_breaking(sb_qkv, sb_norm_g[l])
        x = _out_proj(y_rw, o_sb, sb_gate, w_out_b[l], x, gate, final_g, l == depth - 1)
    return x
```

```python
import functools

import jax
import jax.numpy as jnp
from jax import lax
from jax.experimental import pallas as pl
from jax.experimental.pallas import tpu as pltpu

F32 = jnp.float32
BF16 = jnp.bfloat16

HEAD_DIM = 64
LANES = 128
HEADS_PER_BLOCK = LANES // HEAD_DIM
RWKV_CHUNK = 64
RWKV_STEP_CHUNKS = 4
SB_BLOCK = 128
SB_F32_EXP_UNDERFLOW = 106.0
DECAY_LORA = 64
ICLR_LORA = 64
NORM_EPS = 1e-6
GN_EPS = 64e-5
VMEM_LIMIT_BYTES = 56 * 1024 * 1024

NN = (((1,), (0,)), ((), ()))
NT = (((1,), (1,)), ((), ()))
TN = (((0,), (0,)), ((), ()))


def _dot(a, b, dims=NN):
    return lax.dot_general(a, b, dims, preferred_element_type=F32)


def _split2(x):
    hi = x.astype(BF16)
    lo = (x - hi.astype(F32)).astype(BF16)
    return hi, lo


def _split3(x):
    hi = x.astype(BF16)
    r1 = x - hi.astype(F32)
    mid = r1.astype(BF16)
    lo = (r1 - mid.astype(F32)).astype(BF16)
    return hi, mid, lo


def _dot_exact_rhs(x, rhs_bf16, dims=NN):
    hi, mid, lo = _split3(x)
    return _dot(hi, rhs_bf16, dims) + _dot(mid, rhs_bf16, dims) + _dot(lo, rhs_bf16, dims)


def _dot_x3(a, b_hi, b_lo):
    a_hi, a_lo = _split2(a)
    return _dot(a_hi, b_hi) + _dot(a_lo, b_hi) + _dot(a_hi, b_lo)


def _softplus(u):
    return jnp.maximum(u, 0.0) + jnp.log(1.0 + jnp.exp(-jnp.abs(u)))


def _sigmoid(u):
    return 1.0 / (1.0 + jnp.exp(-u))


def _silu(u):
    return u * _sigmoid(u)


def _head_ones():
    r = lax.broadcasted_iota(jnp.int32, (LANES, LANES), 0)
    c = lax.broadcasted_iota(jnp.int32, (LANES, LANES), 1)
    return jnp.where((r >= HEAD_DIM) == (c >= HEAD_DIM), 1.0, 0.0).astype(BF16)


def _ada_kernel(c_ref, w_ref, b_ref, o_ref):
    c_act = _silu(c_ref[...])
    w_hi, w_lo = _split2(w_ref[0])
    o_ref[0] = _dot_x3(c_act, w_hi, w_lo) + b_ref[0]


def _ada_mod(c, ada_w, ada_b):
    depth, d, d3 = ada_w.shape
    b = c.shape[0]
    rows = 8
    c8 = jnp.zeros((rows, d), F32).at[:b].set(c)
    out = pl.pallas_call(
        _ada_kernel,
        grid=(depth, d3 // d),
        in_specs=[
            pl.BlockSpec((rows, d), lambda l, j: (0, 0)),
            pl.BlockSpec((1, d, d), lambda l, j: (l, 0, j)),
            pl.BlockSpec((1, 1, d), lambda l, j: (l, 0, j)),
        ],
        out_specs=pl.BlockSpec((1, rows, d), lambda l, j: (l, 0, j)),
        out_shape=jax.ShapeDtypeStruct((depth, rows, d3), F32),
        name="ada_mod",
    )(c8, ada_w, ada_b.reshape(depth, 1, d3))
    return out[:, :b]


def _in_proj_kernel(x_ref, g_ref, sc_ref, sh_ref, w_ref, rw_ref, zz_ref, sbq_ref, sbg_ref, *, widths):
    rw_w, zz_w, sbq_w, sbg_w = widths
    x = x_ref[0]
    ms = jnp.mean(x * x, axis=-1, keepdims=True)
    h = x * lax.rsqrt(ms + NORM_EPS) * g_ref[...]
    h = h * (1.0 + sc_ref[0]) + sh_ref[0]
    hb = h.astype(BF16)

    def emit(out_ref, out_col, w_col, width):
        step = 512
        for c0 in range(0, width, step):
            cw = min(step, width - c0)
            res = _dot(hb, w_ref[:, w_col + c0:w_col + c0 + cw])
            out_ref[0, :, out_col + c0:out_col + c0 + cw] = res.astype(out_ref.dtype)

    shift_rkv = rw_w - sbg_w
    emit(rw_ref, 0, 0, shift_rkv)
    emit(zz_ref, 0, shift_rkv, zz_w)
    emit(rw_ref, shift_rkv, shift_rkv + zz_w, sbg_w)
    sb0 = rw_w + zz_w
    emit(sbq_ref, 0, sb0, sbq_w)
    emit(sbg_ref, 0, sb0 + sbq_w, sbg_w)


def _in_proj(x, g, scale, shift, w_bf16, tm):
    b, s, d = x.shape
    width = d // 2
    widths = (4 * width, DECAY_LORA + ICLR_LORA, 3 * width, width)
    n_cols = w_bf16.shape[1]
    assert sum(widths) == n_cols
    return pl.pallas_call(
        functools.partial(_in_proj_kernel, widths=widths),
        grid=(b, s // tm),
        in_specs=[
            pl.BlockSpec((1, tm, d), lambda i, j: (i, j, 0)),
            pl.BlockSpec((1, d), lambda i, j: (0, 0)),
            pl.BlockSpec((1, 1, d), lambda i, j: (i, 0, 0)),
            pl.BlockSpec((1, 1, d), lambda i, j: (i, 0, 0)),
            pl.BlockSpec((d, n_cols), lambda i, j: (0, 0)),
        ],
        out_specs=[
            pl.BlockSpec((1, tm, widths[0]), lambda i, j: (i, j, 0)),
            pl.BlockSpec((1, tm, widths[1]), lambda i, j: (i, j, 0)),
            pl.BlockSpec((1, tm, widths[2]), lambda i, j: (i, j, 0)),
            pl.BlockSpec((1, tm, widths[3]), lambda i, j: (i, j, 0)),
        ],
        out_shape=[
            jax.ShapeDtypeStruct((b, s, widths[0]), F32),
            jax.ShapeDtypeStruct((b, s, widths[1]), F32),
            jax.ShapeDtypeStruct((b, s, widths[2]), BF16),
            jax.ShapeDtypeStruct((b, s, widths[3]), F32),
        ],
        compiler_params=pltpu.CompilerParams(
            dimension_semantics=("parallel", "parallel"), vmem_limit_bytes=VMEM_LIMIT_BYTES),
        name="in_proj",
    )(x, g.reshape(1, d), scale.reshape(b, 1, d), shift.reshape(b, 1, d), w_bf16)


def _rwkv_kernel(rkv_ref, zz_ref, g_ref, mu_ref, muz_ref, wl_ref, vec_ref, y_ref,
                 state_ref, prev_ref, prevz_ref, *, n_blocks):
    L = RWKV_CHUNK
    n_batch, rows = rkv_ref.shape[0], rkv_ref.shape[1]
    n_sub = rows // L
    width = n_blocks * LANES
    t = pl.program_id(0)

    @pl.when(t == 0)
    def _():
        state_ref[...] = jnp.zeros_like(state_ref)
        prev_ref[...] = jnp.zeros_like(prev_ref)
        prevz_ref[...] = jnp.zeros_like(prevz_ref)

    def token_shift(x, carry_ref, b, mu):
        row = lax.broadcasted_iota(jnp.int32, x.shape, 0)
        prev = jnp.where(row == 0, carry_ref[b, 7:8, :], pltpu.roll(x, 1, 0))
        carry_ref[b] = x[rows - 8:rows]
        return x + (prev - x) * mu

    k_k = vec_ref[0:1, :]
    k_a = vec_ref[1:2, :]
    r_k = vec_ref[2:3, :]
    ln_w = vec_ref[3:4, :]
    ln_b = vec_ref[4:5, :]
    w0 = vec_ref[5:6, :]
    a0 = vec_ref[6:7, :]

    lane = lax.broadcasted_iota(jnp.int32, (L, LANES), 1)
    head0 = lane < HEAD_DIM
    tr = lax.broadcasted_iota(jnp.int32, (L, 3 * L), 0)
    tc = lax.broadcasted_iota(jnp.int32, (L, 3 * L), 1)
    tc = jnp.where(tc >= 2 * L, tc - 2 * L, jnp.where(tc >= L, tc - L, tc))
    cum3 = jnp.where(tc <= tr, 1.0, 0.0).astype(BF16)
    pr = lax.broadcasted_iota(jnp.int32, (L, 2 * L), 0)
    pc = lax.broadcasted_iota(jnp.int32, (L, 2 * L), 1)
    pc = jnp.where(pc >= L, pc - L, pc)
    strict = pc < pr
    incl = pc <= pr
    eye2 = jnp.where(pc == pr, 1.0, 0.0)
    br = lax.broadcasted_iota(jnp.int32, (LANES, LANES), 0)
    bc = lax.broadcasted_iota(jnp.int32, (LANES, LANES), 1)
    same_head = (br >= HEAD_DIM) == (bc >= HEAD_DIM)

    def stacked(x):
        xb = x.astype(BF16)
        zero = jnp.zeros_like(xb)
        return jnp.concatenate([jnp.where(head0, xb, zero), jnp.where(head0, zero, xb)], axis=0)

    def head_sum(x):
        s0 = jnp.sum(jnp.where(head0, x, 0.0), axis=1, keepdims=True)
        s1 = jnp.sum(jnp.where(head0, 0.0, x), axis=1, keepdims=True)
        return jnp.where(head0, s0, s1)

    batches = range(n_batch)
    lane_z = lax.broadcasted_iota(jnp.int32, (rows, LANES), 1)
    xs, lora = [], []
    for b in batches:
        xs.append(token_shift(rkv_ref[b], prev_ref, b, mu_ref[...]))
        zs = token_shift(zz_ref[b], prevz_ref, b, muz_ref[...])
        act = jnp.where(lane_z < DECAY_LORA, jnp.tanh(zs), zs)
        a_hi, a_lo = _split2(act)
        lora.append(_dot(jnp.concatenate([a_hi, a_lo, a_hi], axis=1), wl_ref[...]))

    chains = [(b, p, q) for q in range(n_sub) for b in batches for p in range(n_blocks)]
    ids = range(len(chains))
    sls = [slice(p * LANES, (p + 1) * LANES) for _, p, _ in chains]
    rws = [slice(q * L, (q + 1) * L) for _, _, q in chains]
    r = [xs[b][rws[i], p * LANES:(p + 1) * LANES] for i, (b, p, _) in enumerate(chains)]
    k = [xs[b][rws[i], width + p * LANES:width + (p + 1) * LANES] for i, (b, p, _) in enumerate(chains)]
    v = [xs[b][rws[i], 2 * width + p * LANES:2 * width + (p + 1) * LANES] for i, (b, p, _) in enumerate(chains)]
    lw, a = [], []
    for i, (b, p, _) in enumerate(chains):
        w_log = -_softplus(-(w0[:, sls[i]] + lora[b][rws[i], sls[i]])) - 0.5
        lw.append(-jnp.exp(w_log))
        a.append(_sigmoid(a0[:, sls[i]] + lora[b][rws[i], width + p * LANES:width + (p + 1) * LANES]))

    c = [_dot(cum3, jnp.concatenate(_split3(lw[i]), axis=0)) for i in ids]
    kk_raw = [k[i] * k_k[:, sls[i]] for i in ids]
    kk_ss = [head_sum(kk_raw[i] * kk_raw[i]) for i in ids]
    kmod = [k[i] * (1.0 + (a[i] - 1.0) * k_a[:, sls[i]]) for i in ids]
    bonus_dot = [head_sum(r[i] * kmod[i] * r_k[:, sls[i]]) for i in ids]

    lhs, rhs, xa_s, xr_b, v_s, upd, decay_last = [], [], [], [], [], [], []
    for i in ids:
        kk = kk_raw[i] * lax.rsqrt(jnp.maximum(kk_ss[i], 1e-24))
        bvec = kk * a[i]
        c_last = c[i][L - 1:L, :]
        g_in = jnp.exp(c[i])
        g_ex = jnp.exp(c[i] - lw[i])
        g_inv = jnp.exp(-c[i])
        g_rem = jnp.exp(c_last - c[i])
        xa = -kk * g_ex
        xr_b.append((r[i] * g_in).astype(BF16))
        xa_s.append(stacked(xa))
        v_s.append(stacked(v[i]))
        lhs.append(jnp.concatenate([xa.astype(BF16), xr_b[i]], axis=0))
        rhs.append(jnp.concatenate([stacked(bvec * g_inv), stacked(kmod[i] * g_inv)], axis=0))
        upd.append(jnp.concatenate([bvec * g_rem, kmod[i] * g_rem], axis=0).astype(BF16))
        decay_last.append(jnp.exp(c_last))

    gram = [_dot(lhs[i], rhs[i], NT) for i in ids]
    n_p = [jnp.where(strict, gram[i][:L, :2 * L], 0.0) for i in ids]
    a_ak = [jnp.where(strict, gram[i][:L, 2 * L:], 0.0).astype(BF16) for i in ids]
    a_rbk = [jnp.concatenate([jnp.where(incl, gram[i][L:, :2 * L], 0.0),
                              jnp.where(incl, gram[i][L:, 2 * L:], 0.0)], axis=1).astype(BF16) for i in ids]

    n_sq = L.bit_length() - 1
    qmat = [_dot(n_p[i].astype(BF16), stacked(n_p[i])) for i in ids]
    w1 = [_dot(a_ak[i], v_s[i]) for i in ids]
    pmat = [eye2 + n_p[i] for i in ids]
    for step in range(1, n_sq):
        q_s = [stacked(qmat[i]) for i in ids]
        if step < n_sq - 1:
            both = [_dot(jnp.concatenate([pmat[i], qmat[i]], axis=0).astype(BF16), q_s[i]) for i in ids]
            pmat = [pmat[i] + both[i][:L] for i in ids]
            qmat = [both[i][L:] for i in ids]
        else:
            pq = [_dot(pmat[i].astype(BF16), q_s[i]) for i in ids]
            pmat = [pmat[i] + pq[i] for i in ids]

    aw = [_dot(pmat[i].astype(BF16), jnp.concatenate([xa_s[i], stacked(w1[i])], axis=1)) for i in ids]
    n_state = n_batch * n_blocks
    h_t = [state_ref[j] for j in range(n_state)]
    y = []
    for q in range(n_sub):
        sub = range(q * n_state, (q + 1) * n_state)
        s9 = [_dot(jnp.concatenate([aw[i][:, :LANES].astype(BF16), xr_b[i]], axis=0),
                   h_t[i - q * n_state].astype(BF16), NT) for i in sub]
        u = [s9[j][:L] + aw[i][:, LANES:] for j, i in enumerate(sub)]
        y += [s9[j][L:] + _dot(a_rbk[i], jnp.concatenate([stacked(u[j]), v_s[i]], axis=0))
              for j, i in enumerate(sub)]
        for j, i in enumerate(sub):
            uv = jnp.concatenate([u[j], v[i]], axis=0).astype(BF16)
            h_t[j] = h_t[j] * decay_last[i] + jnp.where(same_head, _dot(uv, upd[i], TN), 0.0)
    for j in range(n_state):
        state_ref[j] = h_t[j]

    mean = [head_sum(y[i]) * (1.0 / HEAD_DIM) for i in ids]
    dlt = [y[i] - mean[i] for i in ids]
    var = [head_sum(dlt[i] * dlt[i]) * (1.0 / HEAD_DIM) for i in ids]
    for i, (b, p, _) in enumerate(chains):
        yn = dlt[i] * lax.rsqrt(var[i] + GN_EPS) * ln_w[:, sls[i]] + ln_b[:, sls[i]]
        gate = g_ref[b, rws[i], sls[i]]
        y_ref[b, rws[i], sls[i]] = ((yn + bonus_dot[i] * v[i]) * _silu(gate)).astype(y_ref.dtype)


def _rwkv(rw, zz, mu, w0, w2, a0, a2, k_k, k_a, r_k, ln_w, ln_b):
    b, s, w4 = rw.shape
    width = w4 // 4
    n_blocks = width // LANES
    L = RWKV_CHUNK
    mu_rkv = mu[:3 * width].reshape(1, 3 * width)
    mu_z = mu[3 * width:].reshape(1, DECAY_LORA + ICLR_LORA)
    w_lora = jnp.zeros((DECAY_LORA + ICLR_LORA, 2 * width), F32)
    w_lora = w_lora.at[:DECAY_LORA, :width].set(w2).at[DECAY_LORA:, width:].set(a2)
    wl_hi = w_lora.astype(BF16)
    wl_lo = (w_lora - wl_hi.astype(F32)).astype(BF16)
    wl_cat = jnp.concatenate([wl_hi, wl_hi, wl_lo], axis=0)
    vec = jnp.stack([k_k, k_a, r_k.reshape(width), ln_w, ln_b, w0, a0, jnp.zeros_like(w0)], axis=0)
    rows = L * RWKV_STEP_CHUNKS
    assert s % rows == 0
    return pl.pallas_call(
        functools.partial(_rwkv_kernel, n_blocks=n_blocks),
        grid=(s // rows,),
        in_specs=[
            pl.BlockSpec((b, rows, 3 * width), lambda j: (0, j, 0)),
            pl.BlockSpec((b, rows, LANES), lambda j: (0, j, 0)),
            pl.BlockSpec((b, rows, width), lambda j: (0, j, 3)),
            pl.BlockSpec((1, 3 * width), lambda j: (0, 0)),
            pl.BlockSpec((1, LANES), lambda j: (0, 0)),
            pl.BlockSpec((3 * LANES, 2 * width), lambda j: (0, 0)),
            pl.BlockSpec((8, width), lambda j: (0, 0)),
        ],
        out_specs=pl.BlockSpec((b, rows, width), lambda j: (0, j, 0)),
        out_shape=jax.ShapeDtypeStruct((b, s, width), BF16),
        scratch_shapes=[
            pltpu.VMEM((b * n_blocks, LANES, LANES), F32),
            pltpu.VMEM((b, 8, 3 * width), F32),
            pltpu.VMEM((b, 8, LANES), F32),
        ],
        compiler_params=pltpu.CompilerParams(
            dimension_semantics=("arbitrary",), vmem_limit_bytes=VMEM_LIMIT_BYTES),
        name="rwkv7",
    )(rw, zz, rw, mu_rkv, mu_z, wl_cat, vec)


def _sb_kernel(q_ref, k_ref, v_ref, sg_ref, o_ref, acc_ref, carry_ref, *, n_blocks):
    T = SB_BLOCK
    qi = pl.program_id(0)
    chains = [(b, slice(p * LANES, (p + 1) * LANES)) for b in range(q_ref.shape[0]) for p in range(n_blocks)]
    ids = range(len(chains))
    lane = lax.broadcasted_iota(jnp.int32, (T, LANES), 1)
    head0 = lane < HEAD_DIM
    scale = HEAD_DIM ** -0.5
    qs = []
    for b, cols in chains:
        q = q_ref[b, :, cols]
        zero = jnp.zeros_like(q)
        qs.append(jnp.concatenate([jnp.where(head0, q, zero), jnp.where(head0, zero, q)], axis=0) * scale)

    sr = lax.broadcasted_iota(jnp.int32, (T, T), 0)
    sc = lax.broadcasted_iota(jnp.int32, (T, T), 1)
    tri_ones = jnp.concatenate([jnp.where(sr > sc, 1.0, 0.0), jnp.ones((T, T), F32)], axis=1).astype(BF16)
    mr = lax.broadcasted_iota(jnp.int32, (2 * T, T), 0)
    mc = lax.broadcasted_iota(jnp.int32, (2 * T, T), 1)
    causal = mc < jnp.where(mr >= T, mr - T, mr)

    acc_ref[...] = jnp.zeros_like(acc_ref)
    carry_ref[...] = jnp.zeros_like(carry_ref)

    def tiles(j, diag):
        start = pl.multiple_of(j * T, T)
        z = [_dot(qs[i], k_ref[b, pl.ds(start, T), cols], NT) for i, (b, cols) in enumerate(chains)]
        soft = [jnp.log(1.0 + jnp.exp(-jnp.abs(z[i]))) for i in ids]
        log_1m = [-(jnp.maximum(z[i], 0.0) + soft[i]) for i in ids]
        if diag:
            log_1m = [jnp.where(causal, log_1m[i], 0.0) for i in ids]
        parts = [_split2(log_1m[i]) for i in ids]
        sums = [_dot(parts[i][0], tri_ones) + _dot(parts[i][1], tri_ones) for i in ids]
        for i, (b, cols) in enumerate(chains):
            carry = carry_ref[i]
            log_b = jnp.minimum(z[i], 0.0) - soft[i]
            attn = jnp.exp(log_b + sums[i][:, :T] + carry)
            if diag:
                attn = jnp.where(causal, attn, 0.0)
            acc_ref[i] += _dot(attn.astype(BF16), v_ref[b, pl.ds(start, T), cols])
            carry_ref[i] = carry + sums[i][:, T:]

    def live():
        top = carry_ref[0]
        for i in ids[1:]:
            top = jnp.maximum(top, carry_ref[i])
        return (jnp.max(top) > -SB_F32_EXP_UNDERFLOW).astype(jnp.int32)

    tiles(qi, True)

    def cond(state):
        it, alive = state
        return jnp.logical_and(it < qi, alive > 0)

    def body(state):
        it, _ = state
        tiles(qi - 1 - it, False)
        return it + 1, live()

    lax.while_loop(cond, body, (jnp.int32(0), live()))

    head_ones = _head_ones()
    for i, (b, cols) in enumerate(chains):
        acc = acc_ref[i]
        o = jnp.where(head0, acc[:T], acc[T:])
        ms = _dot_exact_rhs(o * o, head_ones) * (1.0 / HEAD_DIM)
        o_ref[b, :, cols] = o * lax.rsqrt(ms + NORM_EPS) * sg_ref[:, cols]


def _stick_breaking(qkv, sb_g):
    b, s, w3 = qkv.shape
    width = w3 // 3
    n_blocks = width // LANES
    T = SB_BLOCK
    resident = pl.Buffered(1)
    return pl.pallas_call(
        functools.partial(_sb_kernel, n_blocks=n_blocks),
        grid=(s // T,),
        in_specs=[
            pl.BlockSpec((b, T, width), lambda j: (0, j, 0)),
            pl.BlockSpec((b, s, width), lambda j: (0, 0, 1), pipeline_mode=resident),
            pl.BlockSpec((b, s, width), lambda j: (0, 0, 2), pipeline_mode=resident),
            pl.BlockSpec((1, width), lambda j: (0, 0)),
        ],
        out_specs=pl.BlockSpec((b, T, width), lambda j: (0, j, 0)),
        out_shape=jax.ShapeDtypeStruct((b, s, width), F32),
        scratch_shapes=[
            pltpu.VMEM((b * n_blocks, 2 * T, LANES), F32),
            pltpu.VMEM((b * n_blocks, 2 * T, T), F32),
        ],
        compiler_params=pltpu.CompilerParams(
            dimension_semantics=("arbitrary",), vmem_limit_bytes=VMEM_LIMIT_BYTES),
        name="stick_breaking",
    )(qkv, qkv, qkv, sb_g.reshape(1, width))


def _out_proj_kernel(yr_ref, os_ref, sg_ref, w_ref, x_ref, gate_ref, fg_ref, o_ref, perm_ref, *, final_norm):
    half = yr_ref.shape[-1]
    n_qb, sub = os_ref.shape[1], os_ref.shape[2]
    o_tok = os_ref[0].reshape(n_qb * sub, half)
    cols = []
    for cb in range(half // LANES):
        perm_ref[cb] = o_tok[:, cb * LANES:(cb + 1) * LANES]
        cols.append(jnp.concatenate(
            [perm_ref[cb, pl.ds(ii, n_qb, stride=sub), :] for ii in range(sub)], axis=0))
    o_rows = jnp.concatenate(cols, axis=1)
    y_sb = (o_rows * _silu(sg_ref[0])).astype(BF16)
    y = _dot(yr_ref[0], w_ref[:half, :]) + _dot(y_sb, w_ref[half:, :])
    x = x_ref[0] + gate_ref[0] * y
    if final_norm:
        ms = jnp.mean(x * x, axis=-1, keepdims=True)
        x = x * lax.rsqrt(ms + NORM_EPS) * fg_ref[...]
    o_ref[0] = x


def _out_proj(y_rw, o_sb, sb_gate, w_bf16, x, gate, final_g, final_norm):
    b, s, d = x.shape
    half = y_rw.shape[-1]
    n_qb = s // SB_BLOCK
    sub = 8
    tm = sub * n_qb
    o_sb4 = o_sb.reshape(b, n_qb, SB_BLOCK, half)
    return pl.pallas_call(
        functools.partial(_out_proj_kernel, final_norm=final_norm),
        grid=(b, s // tm),
        in_specs=[
            pl.BlockSpec((1, tm, half), lambda i, j: (i, j, 0)),
            pl.BlockSpec((1, n_qb, sub, half), lambda i, j: (i, 0, j, 0)),
            pl.BlockSpec((1, tm, half), lambda i, j: (i, j, 0)),
            pl.BlockSpec((2 * half, d), lambda i, j: (0, 0)),
            pl.BlockSpec((1, tm, d), lambda i, j: (i, j, 0)),
            pl.BlockSpec((1, 1, d), lambda i, j: (i, 0, 0)),
            pl.BlockSpec((1, d), lambda i, j: (0, 0)),
        ],
        out_specs=pl.BlockSpec((1, tm, d), lambda i, j: (i, j, 0)),
        out_shape=jax.ShapeDtypeStruct((b, s, d), F32),
        scratch_shapes=[pltpu.VMEM((half // LANES, tm, LANES), F32)],
        compiler_params=pltpu.CompilerParams(
            dimension_semantics=("parallel", "parallel"), vmem_limit_bytes=VMEM_LIMIT_BYTES),
        name="out_proj",
    )(y_rw, o_sb4, sb_gate, w_bf16, x, gate.reshape(b, 1, d), final_g.reshape(1, d))


def _row_tile(s):
    for tm in (512, 256, 128, 64, 32, 16, 8):
        if s % tm == 0:
            return tm
    raise ValueError(f"sequence length {s} must be a multiple of 8")


def kernel(x, c, norm_g, ada_w, ada_b, w_in, w_out, tshift_mu, decay_w0, decay_w2, iclr_a0, iclr_a2,
           k_k, k_a, r_k, rwkv_ln_w, rwkv_ln_b, sb_norm_g, final_g):
    depth = norm_g.shape[0]
    b, s, d = x.shape
    assert s % SB_BLOCK == 0 and s % RWKV_CHUNK == 0
    tm = _row_tile(s)
    mod = _ada_mod(c, ada_w, ada_b)
    w_in_b = w_in.astype(BF16)
    w_out_b = w_out.astype(BF16)
    for l in range(depth):
        shift, scale, gate = mod[l, :, :d], mod[l, :, d:2 * d], mod[l, :, 2 * d:]
        rw, zz, sb_qkv, sb_gate = _in_proj(x, norm_g[l], scale, shift, w_in_b[l], tm)
        y_rw = _rwkv(rw, zz, tshift_mu[l], decay_w0[l], decay_w2[l], iclr_a0[l], iclr_a2[l],
                     k_k[l], k_a[l], r_k[l], rwkv_ln_w[l], rwkv_ln_b[l])
        o_sb = _stick_breaking(sb_qkv, sb_norm_g[l])
        x = _out_proj(y_rw, o_sb, sb_gate, w_out_b[l], x, gate, final_g, l == depth - 1)
    return x
```

```python
import functools

import jax
import jax.numpy as jnp
from jax import lax
from jax.experimental import pallas as pl
from jax.experimental.pallas import tpu as pltpu

F32 = jnp.float32
BF16 = jnp.bfloat16

HEAD_DIM = 64
LANES = 128
HEADS_PER_BLOCK = LANES // HEAD_DIM
RWKV_CHUNK = 64
RWKV_STEP_CHUNKS = 4
SB_BLOCK = 128
SB_STEP_BLOCKS = 2
SB_F32_EXP_UNDERFLOW = 106.0
DECAY_LORA = 64
ICLR_LORA = 64
NORM_EPS = 1e-6
GN_EPS = 64e-5
VMEM_LIMIT_BYTES = 56 * 1024 * 1024

NN = (((1,), (0,)), ((), ()))
NT = (((1,), (1,)), ((), ()))
TN = (((0,), (0,)), ((), ()))


def _dot(a, b, dims=NN):
    return lax.dot_general(a, b, dims, preferred_element_type=F32)


def _split2(x):
    hi = x.astype(BF16)
    lo = (x - hi.astype(F32)).astype(BF16)
    return hi, lo


def _split3(x):
    hi = x.astype(BF16)
    r1 = x - hi.astype(F32)
    mid = r1.astype(BF16)
    lo = (r1 - mid.astype(F32)).astype(BF16)
    return hi, mid, lo


def _dot_exact_rhs(x, rhs_bf16, dims=NN):
    hi, mid, lo = _split3(x)
    return _dot(hi, rhs_bf16, dims) + _dot(mid, rhs_bf16, dims) + _dot(lo, rhs_bf16, dims)


def _dot_x3(a, b_hi, b_lo):
    a_hi, a_lo = _split2(a)
    return _dot(a_hi, b_hi) + _dot(a_lo, b_hi) + _dot(a_hi, b_lo)


def _softplus(u):
    return jnp.maximum(u, 0.0) + jnp.log(1.0 + jnp.exp(-jnp.abs(u)))


def _sigmoid(u):
    return 1.0 / (1.0 + jnp.exp(-u))


def _silu(u):
    return u * _sigmoid(u)


def _head_ones():
    r = lax.broadcasted_iota(jnp.int32, (LANES, LANES), 0)
    c = lax.broadcasted_iota(jnp.int32, (LANES, LANES), 1)
    return jnp.where((r >= HEAD_DIM) == (c >= HEAD_DIM), 1.0, 0.0).astype(BF16)


def _ada_kernel(c_ref, w_ref, b_ref, o_ref):
    c_act = _silu(c_ref[...])
    w_hi, w_lo = _split2(w_ref[0])
    o_ref[0] = _dot_x3(c_act, w_hi, w_lo) + b_ref[0]


def _ada_mod(c, ada_w, ada_b):
    depth, d, d3 = ada_w.shape
    b = c.shape[0]
    rows = 8
    c8 = jnp.zeros((rows, d), F32).at[:b].set(c)
    out = pl.pallas_call(
        _ada_kernel,
        grid=(depth, d3 // d),
        in_specs=[
            pl.BlockSpec((rows, d), lambda l, j: (0, 0)),
            pl.BlockSpec((1, d, d), lambda l, j: (l, 0, j)),
            pl.BlockSpec((1, 1, d), lambda l, j: (l, 0, j)),
        ],
        out_specs=pl.BlockSpec((1, rows, d), lambda l, j: (l, 0, j)),
        out_shape=jax.ShapeDtypeStruct((depth, rows, d3), F32),
        name="ada_mod",
    )(c8, ada_w, ada_b.reshape(depth, 1, d3))
    return out[:, :b]


def _in_proj_kernel(x_ref, g_ref, sc_ref, sh_ref, w_ref, rw_ref, zz_ref, sbq_ref, sbg_ref, *, widths):
    rw_w, zz_w, sbq_w, sbg_w = widths
    x = x_ref[0]
    ms = jnp.mean(x * x, axis=-1, keepdims=True)
    h = x * lax.rsqrt(ms + NORM_EPS) * g_ref[...]
    h = h * (1.0 + sc_ref[0]) + sh_ref[0]
    hb = h.astype(BF16)

    def emit(out_ref, out_col, w_col, width):
        step = 512
        for c0 in range(0, width, step):
            cw = min(step, width - c0)
            res = _dot(hb, w_ref[:, w_col + c0:w_col + c0 + cw])
            out_ref[0, :, out_col + c0:out_col + c0 + cw] = res.astype(out_ref.dtype)

    shift_rkv = rw_w - sbg_w
    emit(rw_ref, 0, 0, shift_rkv)
    emit(zz_ref, 0, shift_rkv, zz_w)
    emit(rw_ref, shift_rkv, shift_rkv + zz_w, sbg_w)
    sb0 = rw_w + zz_w
    emit(sbq_ref, 0, sb0, sbq_w)
    emit(sbg_ref, 0, sb0 + sbq_w, sbg_w)


def _in_proj(x, g, scale, shift, w_bf16, tm):
    b, s, d = x.shape
    width = d // 2
    widths = (4 * width, DECAY_LORA + ICLR_LORA, 3 * width, width)
    n_cols = w_bf16.shape[1]
    assert sum(widths) == n_cols
    return pl.pallas_call(
        functools.partial(_in_proj_kernel, widths=widths),
        grid=(b, s // tm),
        in_specs=[
            pl.BlockSpec((1, tm, d), lambda i, j: (i, j, 0)),
            pl.BlockSpec((1, d), lambda i, j: (0, 0)),
            pl.BlockSpec((1, 1, d), lambda i, j: (i, 0, 0)),
            pl.BlockSpec((1, 1, d), lambda i, j: (i, 0, 0)),
            pl.BlockSpec((d, n_cols), lambda i, j: (0, 0)),
        ],
        out_specs=[
            pl.BlockSpec((1, tm, widths[0]), lambda i, j: (i, j, 0)),
            pl.BlockSpec((1, tm, widths[1]), lambda i, j: (i, j, 0)),
            pl.BlockSpec((1, tm, widths[2]), lambda i, j: (i, j, 0)),
            pl.BlockSpec((1, tm, widths[3]), lambda i, j: (i, j, 0)),
        ],
        out_shape=[
            jax.ShapeDtypeStruct((b, s, widths[0]), F32),
            jax.ShapeDtypeStruct((b, s, widths[1]), F32),
            jax.ShapeDtypeStruct((b, s, widths[2]), BF16),
            jax.ShapeDtypeStruct((b, s, widths[3]), F32),
        ],
        compiler_params=pltpu.CompilerParams(
            dimension_semantics=("parallel", "parallel"), vmem_limit_bytes=VMEM_LIMIT_BYTES),
        name="in_proj",
    )(x, g.reshape(1, d), scale.reshape(b, 1, d), shift.reshape(b, 1, d), w_bf16)


def _rwkv_kernel(rkv_ref, zz_ref, g_ref, mu_ref, muz_ref, wl_ref, vec_ref, y_ref,
                 state_ref, prev_ref, prevz_ref, *, n_blocks):
    L = RWKV_CHUNK
    n_batch, rows = rkv_ref.shape[0], rkv_ref.shape[1]
    n_sub = rows // L
    width = n_blocks * LANES
    t = pl.program_id(0)

    @pl.when(t == 0)
    def _():
        state_ref[...] = jnp.zeros_like(state_ref)
        prev_ref[...] = jnp.zeros_like(prev_ref)
        prevz_ref[...] = jnp.zeros_like(prevz_ref)

    def token_shift(x, carry_ref, b, mu):
        row = lax.broadcasted_iota(jnp.int32, x.shape, 0)
        prev = jnp.where(row == 0, carry_ref[b, 7:8, :], pltpu.roll(x, 1, 0))
        carry_ref[b] = x[rows - 8:rows]
        return x + (prev - x) * mu

    k_k = vec_ref[0:1, :]
    k_a = vec_ref[1:2, :]
    r_k = vec_ref[2:3, :]
    ln_w = vec_ref[3:4, :]
    ln_b = vec_ref[4:5, :]
    w0 = vec_ref[5:6, :]
    a0 = vec_ref[6:7, :]

    lane = lax.broadcasted_iota(jnp.int32, (L, LANES), 1)
    head0 = lane < HEAD_DIM
    tr = lax.broadcasted_iota(jnp.int32, (L, 3 * L), 0)
    tc = lax.broadcasted_iota(jnp.int32, (L, 3 * L), 1)
    tc = jnp.where(tc >= 2 * L, tc - 2 * L, jnp.where(tc >= L, tc - L, tc))
    cum3 = jnp.where(tc <= tr, 1.0, 0.0).astype(BF16)
    pr = lax.broadcasted_iota(jnp.int32, (L, 2 * L), 0)
    pc = lax.broadcasted_iota(jnp.int32, (L, 2 * L), 1)
    pc = jnp.where(pc >= L, pc - L, pc)
    strict = pc < pr
    incl = pc <= pr
    eye2 = jnp.where(pc == pr, 1.0, 0.0)
    br = lax.broadcasted_iota(jnp.int32, (LANES, LANES), 0)
    bc = lax.broadcasted_iota(jnp.int32, (LANES, LANES), 1)
    same_head = (br >= HEAD_DIM) == (bc >= HEAD_DIM)

    def stacked(x):
        xb = x.astype(BF16)
        zero = jnp.zeros_like(xb)
        return jnp.concatenate([jnp.where(head0, xb, zero), jnp.where(head0, zero, xb)], axis=0)

    def head_sum(x):
        s0 = jnp.sum(jnp.where(head0, x, 0.0), axis=1, keepdims=True)
        s1 = jnp.sum(jnp.where(head0, 0.0, x), axis=1, keepdims=True)
        return jnp.where(head0, s0, s1)

    batches = range(n_batch)
    lane_z = lax.broadcasted_iota(jnp.int32, (rows, LANES), 1)
    xs, lora = [], []
    for b in batches:
        xs.append(token_shift(rkv_ref[b], prev_ref, b, mu_ref[...]))
        zs = token_shift(zz_ref[b], prevz_ref, b, muz_ref[...])
        act = jnp.where(lane_z < DECAY_LORA, jnp.tanh(zs), zs)
        a_hi, a_lo = _split2(act)
        lora.append(_dot(jnp.concatenate([a_hi, a_lo, a_hi], axis=1), wl_ref[...]))

    chains = [(b, p, q) for q in range(n_sub) for b in batches for p in range(n_blocks)]
    ids = range(len(chains))
    sls = [slice(p * LANES, (p + 1) * LANES) for _, p, _ in chains]
    rws = [slice(q * L, (q + 1) * L) for _, _, q in chains]
    r = [xs[b][rws[i], p * LANES:(p + 1) * LANES] for i, (b, p, _) in enumerate(chains)]
    k = [xs[b][rws[i], width + p * LANES:width + (p + 1) * LANES] for i, (b, p, _) in enumerate(chains)]
    v = [xs[b][rws[i], 2 * width + p * LANES:2 * width + (p + 1) * LANES] for i, (b, p, _) in enumerate(chains)]
    lw, a = [], []
    for i, (b, p, _) in enumerate(chains):
        w_log = -_softplus(-(w0[:, sls[i]] + lora[b][rws[i], sls[i]])) - 0.5
        lw.append(-jnp.exp(w_log))
        a.append(_sigmoid(a0[:, sls[i]] + lora[b][rws[i], width + p * LANES:width + (p + 1) * LANES]))

    c = [_dot(cum3, jnp.concatenate(_split3(lw[i]), axis=0)) for i in ids]
    kk_raw = [k[i] * k_k[:, sls[i]] for i in ids]
    kk_ss = [head_sum(kk_raw[i] * kk_raw[i]) for i in ids]
    kmod = [k[i] * (1.0 + (a[i] - 1.0) * k_a[:, sls[i]]) for i in ids]
    bonus_dot = [head_sum(r[i] * kmod[i] * r_k[:, sls[i]]) for i in ids]

    lhs, rhs, xa_s, xr_b, v_s, upd, decay_last = [], [], [], [], [], [], []
    for i in ids:
        kk = kk_raw[i] * lax.rsqrt(jnp.maximum(kk_ss[i], 1e-24))
        bvec = kk * a[i]
        c_last = c[i][L - 1:L, :]
        g_in = jnp.exp(c[i])
        g_ex = jnp.exp(c[i] - lw[i])
        g_inv = jnp.exp(-c[i])
        g_rem = jnp.exp(c_last - c[i])
        xa = -kk * g_ex
        xr_b.append((r[i] * g_in).astype(BF16))
        xa_s.append(stacked(xa))
        v_s.append(stacked(v[i]))
        lhs.append(jnp.concatenate([xa.astype(BF16), xr_b[i]], axis=0))
        rhs.append(jnp.concatenate([stacked(bvec * g_inv), stacked(kmod[i] * g_inv)], axis=0))
        upd.append(jnp.concatenate([bvec * g_rem, kmod[i] * g_rem], axis=0).astype(BF16))
        decay_last.append(jnp.exp(c_last))

    gram = [_dot(lhs[i], rhs[i], NT) for i in ids]
    n_p = [jnp.where(strict, gram[i][:L, :2 * L], 0.0) for i in ids]
    a_ak = [jnp.where(strict, gram[i][:L, 2 * L:], 0.0).astype(BF16) for i in ids]
    a_rbk = [jnp.concatenate([jnp.where(incl, gram[i][L:, :2 * L], 0.0),
                              jnp.where(incl, gram[i][L:, 2 * L:], 0.0)], axis=1).astype(BF16) for i in ids]

    n_sq = L.bit_length() - 1
    qmat = [_dot(n_p[i].astype(BF16), stacked(n_p[i])) for i in ids]
    w1 = [_dot(a_ak[i], v_s[i]) for i in ids]
    pmat = [eye2 + n_p[i] for i in ids]
    for step in range(1, n_sq):
        q_s = [stacked(qmat[i]) for i in ids]
        if step < n_sq - 1:
            both = [_dot(jnp.concatenate([pmat[i], qmat[i]], axis=0).astype(BF16), q_s[i]) for i in ids]
            pmat = [pmat[i] + both[i][:L] for i in ids]
            qmat = [both[i][L:] for i in ids]
        else:
            pq = [_dot(pmat[i].astype(BF16), q_s[i]) for i in ids]
            pmat = [pmat[i] + pq[i] for i in ids]

    aw = [_dot(pmat[i].astype(BF16), jnp.concatenate([xa_s[i], stacked(w1[i])], axis=1)) for i in ids]
    n_state = n_batch * n_blocks
    h_t = [state_ref[j] for j in range(n_state)]
    y = []
    for q in range(n_sub):
        sub = range(q * n_state, (q + 1) * n_state)
        s9 = [_dot(jnp.concatenate([aw[i][:, :LANES].astype(BF16), xr_b[i]], axis=0),
                   h_t[i - q * n_state].astype(BF16), NT) for i in sub]
        u = [s9[j][:L] + aw[i][:, LANES:] for j, i in enumerate(sub)]
        y += [s9[j][L:] + _dot(a_rbk[i], jnp.concatenate([stacked(u[j]), v_s[i]], axis=0))
              for j, i in enumerate(sub)]
        for j, i in enumerate(sub):
            uv = jnp.concatenate([u[j], v[i]], axis=0).astype(BF16)
            h_t[j] = h_t[j] * decay_last[i] + jnp.where(same_head, _dot(uv, upd[i], TN), 0.0)
    for j in range(n_state):
        state_ref[j] = h_t[j]

    mean = [head_sum(y[i]) * (1.0 / HEAD_DIM) for i in ids]
    dlt = [y[i] - mean[i] for i in ids]
    var = [head_sum(dlt[i] * dlt[i]) * (1.0 / HEAD_DIM) for i in ids]
    for i, (b, p, _) in enumerate(chains):
        yn = dlt[i] * lax.rsqrt(var[i] + GN_EPS) * ln_w[:, sls[i]] + ln_b[:, sls[i]]
        gate = g_ref[b, rws[i], sls[i]]
        y_ref[b, rws[i], sls[i]] = ((yn + bonus_dot[i] * v[i]) * _silu(gate)).astype(y_ref.dtype)


def _rwkv(rw, zz, mu, w0, w2, a0, a2, k_k, k_a, r_k, ln_w, ln_b):
    b, s, w4 = rw.shape
    width = w4 // 4
    n_blocks = width // LANES
    L = RWKV_CHUNK
    mu_rkv = mu[:3 * width].reshape(1, 3 * width)
    mu_z = mu[3 * width:].reshape(1, DECAY_LORA + ICLR_LORA)
    w_lora = jnp.zeros((DECAY_LORA + ICLR_LORA, 2 * width), F32)
    w_lora = w_lora.at[:DECAY_LORA, :width].set(w2).at[DECAY_LORA:, width:].set(a2)
    wl_hi = w_lora.astype(BF16)
    wl_lo = (w_lora - wl_hi.astype(F32)).astype(BF16)
    wl_cat = jnp.concatenate([wl_hi, wl_hi, wl_lo], axis=0)
    vec = jnp.stack([k_k, k_a, r_k.reshape(width), ln_w, ln_b, w0, a0, jnp.zeros_like(w0)], axis=0)
    rows = L * RWKV_STEP_CHUNKS
    assert s % rows == 0
    return pl.pallas_call(
        functools.partial(_rwkv_kernel, n_blocks=n_blocks),
        grid=(s // rows,),
        in_specs=[
            pl.BlockSpec((b, rows, 3 * width), lambda j: (0, j, 0)),
            pl.BlockSpec((b, rows, LANES), lambda j: (0, j, 0)),
            pl.BlockSpec((b, rows, width), lambda j: (0, j, 3)),
            pl.BlockSpec((1, 3 * width), lambda j: (0, 0)),
            pl.BlockSpec((1, LANES), lambda j: (0, 0)),
            pl.BlockSpec((3 * LANES, 2 * width), lambda j: (0, 0)),
            pl.BlockSpec((8, width), lambda j: (0, 0)),
        ],
        out_specs=pl.BlockSpec((b, rows, width), lambda j: (0, j, 0)),
        out_shape=jax.ShapeDtypeStruct((b, s, width), BF16),
        scratch_shapes=[
            pltpu.VMEM((b * n_blocks, LANES, LANES), F32),
            pltpu.VMEM((b, 8, 3 * width), F32),
            pltpu.VMEM((b, 8, LANES), F32),
        ],
        compiler_params=pltpu.CompilerParams(
            dimension_semantics=("arbitrary",), vmem_limit_bytes=VMEM_LIMIT_BYTES),
        name="rwkv7",
    )(rw, zz, rw, mu_rkv, mu_z, wl_cat, vec)


def _sb_kernel(q_ref, k_ref, v_ref, sg_ref, o_ref, acc_ref, carry_ref, *, n_blocks):
    T = SB_BLOCK
    n_sub = q_ref.shape[1] // T
    step = pl.program_id(0)
    chains = [(h, b, slice(p * LANES, (p + 1) * LANES))
              for h in range(n_sub) for b in range(q_ref.shape[0]) for p in range(n_blocks)]
    ids = range(len(chains))
    lane = lax.broadcasted_iota(jnp.int32, (T, LANES), 1)
    head0 = lane < HEAD_DIM
    scale = HEAD_DIM ** -0.5
    qs = []
    for h, b, cols in chains:
        q = q_ref[b, h * T:(h + 1) * T, cols]
        zero = jnp.zeros_like(q)
        qs.append(jnp.concatenate([jnp.where(head0, q, zero), jnp.where(head0, zero, q)], axis=0) * scale)

    sr = lax.broadcasted_iota(jnp.int32, (T, T), 0)
    sc = lax.broadcasted_iota(jnp.int32, (T, T), 1)
    tri_ones = jnp.concatenate([jnp.where(sr > sc, 1.0, 0.0), jnp.ones((T, T), F32)], axis=1).astype(BF16)
    mr = lax.broadcasted_iota(jnp.int32, (2 * T, T), 0)
    mc = lax.broadcasted_iota(jnp.int32, (2 * T, T), 1)
    causal = mc < jnp.where(mr >= T, mr - T, mr)

    def tiles(sel, back, diag):
        starts = [pl.multiple_of((step * n_sub + h - back) * T, T) for h in range(n_sub)]
        z = {i: _dot(qs[i], k_ref[chains[i][1], pl.ds(starts[chains[i][0]], T), chains[i][2]], NT)
             for i in sel}
        soft = {i: jnp.log(1.0 + jnp.exp(-jnp.abs(z[i]))) for i in sel}
        log_1m = {i: -(jnp.maximum(z[i], 0.0) + soft[i]) for i in sel}
        if diag:
            log_1m = {i: jnp.where(causal, log_1m[i], 0.0) for i in sel}
        parts = {i: _split2(log_1m[i]) for i in sel}
        sums = {i: _dot(parts[i][0], tri_ones) + _dot(parts[i][1], tri_ones) for i in sel}
        for i in sel:
            h, b, cols = chains[i]
            log_b = jnp.minimum(z[i], 0.0) - soft[i]
            if diag:
                attn = jnp.where(causal, jnp.exp(log_b + sums[i][:, :T]), 0.0)
                acc_ref[i] = _dot(attn.astype(BF16), v_ref[b, pl.ds(starts[h], T), cols])
                carry_ref[i] = sums[i][:, T:]
            else:
                carry = carry_ref[i]
                attn = jnp.exp(log_b + sums[i][:, :T] + carry)
                acc_ref[i] += _dot(attn.astype(BF16), v_ref[b, pl.ds(starts[h], T), cols])
                carry_ref[i] = carry + sums[i][:, T:]

    def live():
        top = carry_ref[0]
        for i in ids[1:]:
            top = jnp.maximum(top, carry_ref[i])
        return (jnp.max(top) > -SB_F32_EXP_UNDERFLOW).astype(jnp.int32)

    tiles(ids, 0, True)
    n_common = step * n_sub

    def cond(state):
        it, alive = state
        return jnp.logical_and(it < n_common, alive > 0)

    def body(state):
        it, _ = state
        tiles(ids, it + 1, False)
        return it + 1, live()

    _, alive = lax.while_loop(cond, body, (jnp.int32(0), live()))

    for extra in range(1, n_sub):
        @pl.when(alive > 0)
        def _():
            tiles([i for i in ids if chains[i][0] >= extra], n_common + extra, False)

    head_ones = _head_ones()
    for i, (h, b, cols) in enumerate(chains):
        acc = acc_ref[i]
        o = jnp.where(head0, acc[:T], acc[T:])
        ms = _dot_exact_rhs(o * o, head_ones) * (1.0 / HEAD_DIM)
        o_ref[b, h * T:(h + 1) * T, cols] = o * lax.rsqrt(ms + NORM_EPS) * sg_ref[:, cols]


def _stick_breaking(qkv, sb_g):
    b, s, w3 = qkv.shape
    width = w3 // 3
    n_blocks = width // LANES
    T = SB_BLOCK
    resident = pl.Buffered(1)
    rows = T * SB_STEP_BLOCKS
    assert s % rows == 0
    n_chains = SB_STEP_BLOCKS * b * n_blocks
    return pl.pallas_call(
        functools.partial(_sb_kernel, n_blocks=n_blocks),
        grid=(s // rows,),
        in_specs=[
            pl.BlockSpec((b, rows, width), lambda j: (0, j, 0)),
            pl.BlockSpec((b, s, width), lambda j: (0, 0, 1), pipeline_mode=resident),
            pl.BlockSpec((b, s, width), lambda j: (0, 0, 2), pipeline_mode=resident),
            pl.BlockSpec((1, width), lambda j: (0, 0)),
        ],
        out_specs=pl.BlockSpec((b, rows, width), lambda j: (0, j, 0)),
        out_shape=jax.ShapeDtypeStruct((b, s, width), F32),
        scratch_shapes=[
            pltpu.VMEM((n_chains, 2 * T, LANES), F32),
            pltpu.VMEM((n_chains, 2 * T, T), F32),
        ],
        compiler_params=pltpu.CompilerParams(
            dimension_semantics=("arbitrary",), vmem_limit_bytes=VMEM_LIMIT_BYTES),
        name="stick_breaking",
    )(qkv, qkv, qkv, sb_g.reshape(1, width))


def _out_proj_kernel(yr_ref, os_ref, sg_ref, w_ref, x_ref, gate_ref, fg_ref, o_ref, perm_ref, *, final_norm):
    half = yr_ref.shape[-1]
    n_qb, sub = os_ref.shape[1], os_ref.shape[2]
    o_tok = os_ref[0].reshape(n_qb * sub, half)
    cols = []
    for cb in range(half // LANES):
        perm_ref[cb] = o_tok[:, cb * LANES:(cb + 1) * LANES]
        cols.append(jnp.concatenate(
            [perm_ref[cb, pl.ds(ii, n_qb, stride=sub), :] for ii in range(sub)], axis=0))
    o_rows = jnp.concatenate(cols, axis=1)
    y_sb = (o_rows * _silu(sg_ref[0])).astype(BF16)
    y = _dot(yr_ref[0], w_ref[:half, :]) + _dot(y_sb, w_ref[half:, :])
    x = x_ref[0] + gate_ref[0] * y
    if final_norm:
        ms = jnp.mean(x * x, axis=-1, keepdims=True)
        x = x * lax.rsqrt(ms + NORM_EPS) * fg_ref[...]
    o_ref[0] = x


def _out_proj(y_rw, o_sb, sb_gate, w_bf16, x, gate, final_g, final_norm):
    b, s, d = x.shape
    half = y_rw.shape[-1]
    n_qb = s // SB_BLOCK
    sub = 8
    tm = sub * n_qb
    o_sb4 = o_sb.reshape(b, n_qb, SB_BLOCK, half)
    return pl.pallas_call(
        functools.partial(_out_proj_kernel, final_norm=final_norm),
        grid=(b, s // tm),
        in_specs=[
            pl.BlockSpec((1, tm, half), lambda i, j: (i, j, 0)),
            pl.BlockSpec((1, n_qb, sub, half), lambda i, j: (i, 0, j, 0)),
            pl.BlockSpec((1, tm, half), lambda i, j: (i, j, 0)),
            pl.BlockSpec((2 * half, d), lambda i, j: (0, 0)),
            pl.BlockSpec((1, tm, d), lambda i, j: (i, j, 0)),
            pl.BlockSpec((1, 1, d), lambda i, j: (i, 0, 0)),
            pl.BlockSpec((1, d), lambda i, j: (0, 0)),
        ],
        out_specs=pl.BlockSpec((1, tm, d), lambda i, j: (i, j, 0)),
        out_shape=jax.ShapeDtypeStruct((b, s, d), F32),
        scratch_shapes=[pltpu.VMEM((half // LANES, tm, LANES), F32)],
        compiler_params=pltpu.CompilerParams(
            dimension_semantics=("parallel", "parallel"), vmem_limit_bytes=VMEM_LIMIT_BYTES),
        name="out_proj",
    )(y_rw, o_sb4, sb_gate, w_bf16, x, gate.reshape(b, 1, d), final_g.reshape(1, d))


def _row_tile(s):
    for tm in (512, 256, 128, 64, 32, 16, 8):
        if s % tm == 0:
            return tm
    raise ValueError(f"sequence length {s} must be a multiple of 8")


def kernel(x, c, norm_g, ada_w, ada_b, w_in, w_out, tshift_mu, decay_w0, decay_w2, iclr_a0, iclr_a2,
           k_k, k_a, r_k, rwkv_ln_w, rwkv_ln_b, sb_norm_g, final_g):
    depth = norm_g.shape[0]
    b, s, d = x.shape
    assert s % SB_BLOCK == 0 and s % RWKV_CHUNK == 0
    tm = _row_tile(s)
    mod = _ada_mod(c, ada_w, ada_b)
    w_in_b = w_in.astype(BF16)
    w_out_b = w_out.astype(BF16)
    for l in range(depth):
        shift, scale, gate = mod[l, :, :d], mod[l, :, d:2 * d], mod[l, :, 2 * d:]
        rw, zz, sb_qkv, sb_gate = _in_proj(x, norm_g[l], scale, shift, w_in_b[l], tm)
        y_rw = _rwkv(rw, zz, tshift_mu[l], decay_w0[l], decay_w2[l], iclr_a0[l], iclr_a2[l],
                     k_k[l], k_a[l], r_k[l], rwkv_ln_w[l], rwkv_ln_b[l])
        o_sb = _stick_breaking(sb_qkv, sb_norm_g[l])
        x = _out_proj(y_rw, o_sb, sb_gate, w_out_b[l], x, gate, final_g, l == depth - 1)
    return x
```

```python
import functools

import jax
import jax.numpy as jnp
from jax import lax
from jax.experimental import pallas as pl
from jax.experimental.pallas import tpu as pltpu

F32 = jnp.float32
BF16 = jnp.bfloat16

HEAD_DIM = 64
LANES = 128
HEADS_PER_BLOCK = LANES // HEAD_DIM
RWKV_CHUNK = 64
RWKV_STEP_CHUNKS = 4
SB_BLOCK = 128
SB_STEP_BLOCKS = 2
SB_F32_EXP_UNDERFLOW = 106.0
DECAY_LORA = 64
ICLR_LORA = 64
NORM_EPS = 1e-6
GN_EPS = 64e-5
VMEM_LIMIT_BYTES = 56 * 1024 * 1024

NN = (((1,), (0,)), ((), ()))
NT = (((1,), (1,)), ((), ()))
TN = (((0,), (0,)), ((), ()))


def _dot(a, b, dims=NN):
    return lax.dot_general(a, b, dims, preferred_element_type=F32)


def _split2(x):
    hi = x.astype(BF16)
    lo = (x - hi.astype(F32)).astype(BF16)
    return hi, lo


def _split3(x):
    hi = x.astype(BF16)
    r1 = x - hi.astype(F32)
    mid = r1.astype(BF16)
    lo = (r1 - mid.astype(F32)).astype(BF16)
    return hi, mid, lo


def _dot_exact_rhs(x, rhs_bf16, dims=NN):
    hi, mid, lo = _split3(x)
    return _dot(hi, rhs_bf16, dims) + _dot(mid, rhs_bf16, dims) + _dot(lo, rhs_bf16, dims)


def _dot_x3(a, b_hi, b_lo):
    a_hi, a_lo = _split2(a)
    return _dot(a_hi, b_hi) + _dot(a_lo, b_hi) + _dot(a_hi, b_lo)


def _softplus(u):
    return jnp.maximum(u, 0.0) + jnp.log(1.0 + jnp.exp(-jnp.abs(u)))


def _sigmoid(u):
    return 1.0 / (1.0 + jnp.exp(-u))


def _silu(u):
    return u * _sigmoid(u)


def _head_ones():
    r = lax.broadcasted_iota(jnp.int32, (LANES, LANES), 0)
    c = lax.broadcasted_iota(jnp.int32, (LANES, LANES), 1)
    return jnp.where((r >= HEAD_DIM) == (c >= HEAD_DIM), 1.0, 0.0).astype(BF16)


def _ada_kernel(c_ref, w_ref, b_ref, o_ref):
    c_act = _silu(c_ref[...])
    w_hi, w_lo = _split2(w_ref[0])
    o_ref[0] = _dot_x3(c_act, w_hi, w_lo) + b_ref[0]


def _ada_mod(c, ada_w, ada_b):
    depth, d, d3 = ada_w.shape
    b = c.shape[0]
    rows = 8
    c8 = jnp.zeros((rows, d), F32).at[:b].set(c)
    out = pl.pallas_call(
        _ada_kernel,
        grid=(depth, d3 // d),
        in_specs=[
            pl.BlockSpec((rows, d), lambda l, j: (0, 0)),
            pl.BlockSpec((1, d, d), lambda l, j: (l, 0, j)),
            pl.BlockSpec((1, 1, d), lambda l, j: (l, 0, j)),
        ],
        out_specs=pl.BlockSpec((1, rows, d), lambda l, j: (l, 0, j)),
        out_shape=jax.ShapeDtypeStruct((depth, rows, d3), F32),
        name="ada_mod",
    )(c8, ada_w, ada_b.reshape(depth, 1, d3))
    return out[:, :b]


def _in_proj_kernel(x_ref, g_ref, sc_ref, sh_ref, w_ref, rw_ref, zz_ref, sbq_ref, sbg_ref, *, widths):
    rw_w, zz_w, sbq_w, sbg_w = widths
    x = x_ref[0]
    ms = jnp.mean(x * x, axis=-1, keepdims=True)
    h = x * lax.rsqrt(ms + NORM_EPS) * g_ref[...]
    h = h * (1.0 + sc_ref[0]) + sh_ref[0]
    hb = h.astype(BF16)

    def emit(out_ref, out_col, w_col, width):
        step = 512
        for c0 in range(0, width, step):
            cw = min(step, width - c0)
            res = _dot(hb, w_ref[:, w_col + c0:w_col + c0 + cw])
            out_ref[0, :, out_col + c0:out_col + c0 + cw] = res.astype(out_ref.dtype)

    shift_rkv = rw_w - sbg_w
    emit(rw_ref, 0, 0, shift_rkv)
    emit(zz_ref, 0, shift_rkv, zz_w)
    emit(rw_ref, shift_rkv, shift_rkv + zz_w, sbg_w)
    sb0 = rw_w + zz_w
    emit(sbq_ref, 0, sb0, sbq_w)
    emit(sbg_ref, 0, sb0 + sbq_w, sbg_w)


def _in_proj(x, g, scale, shift, w_bf16, tm):
    b, s, d = x.shape
    width = d // 2
    widths = (4 * width, DECAY_LORA + ICLR_LORA, 3 * width, width)
    n_cols = w_bf16.shape[1]
    assert sum(widths) == n_cols
    return pl.pallas_call(
        functools.partial(_in_proj_kernel, widths=widths),
        grid=(b, s // tm),
        in_specs=[
            pl.BlockSpec((1, tm, d), lambda i, j: (i, j, 0)),
            pl.BlockSpec((1, d), lambda i, j: (0, 0)),
            pl.BlockSpec((1, 1, d), lambda i, j: (i, 0, 0)),
            pl.BlockSpec((1, 1, d), lambda i, j: (i, 0, 0)),
            pl.BlockSpec((d, n_cols), lambda i, j: (0, 0)),
        ],
        out_specs=[
            pl.BlockSpec((1, tm, widths[0]), lambda i, j: (i, j, 0)),
            pl.BlockSpec((1, tm, widths[1]), lambda i, j: (i, j, 0)),
            pl.BlockSpec((1, tm, widths[2]), lambda i, j: (i, j, 0)),
            pl.BlockSpec((1, tm, widths[3]), lambda i, j: (i, j, 0)),
        ],
        out_shape=[
            jax.ShapeDtypeStruct((b, s, widths[0]), F32),
            jax.ShapeDtypeStruct((b, s, widths[1]), F32),
            jax.ShapeDtypeStruct((b, s, widths[2]), BF16),
            jax.ShapeDtypeStruct((b, s, widths[3]), F32),
        ],
        compiler_params=pltpu.CompilerParams(
            dimension_semantics=("parallel", "parallel"), vmem_limit_bytes=VMEM_LIMIT_BYTES),
        name="in_proj",
    )(x, g.reshape(1, d), scale.reshape(b, 1, d), shift.reshape(b, 1, d), w_bf16)


def _rwkv_kernel(rkv_ref, zz_ref, g_ref, mu_ref, muz_ref, wl_ref, vec_ref, y_ref,
                 state_ref, prev_ref, prevz_ref, *, n_blocks):
    L = RWKV_CHUNK
    n_batch, rows = rkv_ref.shape[0], rkv_ref.shape[1]
    n_sub = rows // L
    width = n_blocks * LANES
    t = pl.program_id(0)

    @pl.when(t == 0)
    def _():
        state_ref[...] = jnp.zeros_like(state_ref)
        prev_ref[...] = jnp.zeros_like(prev_ref)
        prevz_ref[...] = jnp.zeros_like(prevz_ref)

    def token_shift(x, carry_ref, b, mu):
        row = lax.broadcasted_iota(jnp.int32, x.shape, 0)
        prev = jnp.where(row == 0, carry_ref[b, 7:8, :], pltpu.roll(x, 1, 0))
        carry_ref[b] = x[rows - 8:rows]
        return x + (prev - x) * mu

    k_k = vec_ref[0:1, :]
    k_a = vec_ref[1:2, :]
    r_k = vec_ref[2:3, :]
    ln_w = vec_ref[3:4, :]
    ln_b = vec_ref[4:5, :]
    w0 = vec_ref[5:6, :]
    a0 = vec_ref[6:7, :]

    lane = lax.broadcasted_iota(jnp.int32, (L, LANES), 1)
    head0 = lane < HEAD_DIM
    tr = lax.broadcasted_iota(jnp.int32, (L, 3 * L), 0)
    tc = lax.broadcasted_iota(jnp.int32, (L, 3 * L), 1)
    tc = jnp.where(tc >= 2 * L, tc - 2 * L, jnp.where(tc >= L, tc - L, tc))
    cum3 = jnp.where(tc <= tr, 1.0, 0.0).astype(BF16)
    pr = lax.broadcasted_iota(jnp.int32, (L, 2 * L), 0)
    pc = lax.broadcasted_iota(jnp.int32, (L, 2 * L), 1)
    pc = jnp.where(pc >= L, pc - L, pc)
    strict = pc < pr
    incl = pc <= pr
    eye2 = jnp.where(pc == pr, 1.0, 0.0)
    br = lax.broadcasted_iota(jnp.int32, (LANES, LANES), 0)
    bc = lax.broadcasted_iota(jnp.int32, (LANES, LANES), 1)
    same_head = (br >= HEAD_DIM) == (bc >= HEAD_DIM)

    def stacked(x):
        xb = x.astype(BF16)
        zero = jnp.zeros_like(xb)
        return jnp.concatenate([jnp.where(head0, xb, zero), jnp.where(head0, zero, xb)], axis=0)

    def head_sum(x):
        s0 = jnp.sum(jnp.where(head0, x, 0.0), axis=1, keepdims=True)
        s1 = jnp.sum(jnp.where(head0, 0.0, x), axis=1, keepdims=True)
        return jnp.where(head0, s0, s1)

    batches = range(n_batch)
    lane_z = lax.broadcasted_iota(jnp.int32, (rows, LANES), 1)
    xs, lora = [], []
    for b in batches:
        xs.append(token_shift(rkv_ref[b], prev_ref, b, mu_ref[...]))
        zs = token_shift(zz_ref[b], prevz_ref, b, muz_ref[...])
        act = jnp.where(lane_z < DECAY_LORA, jnp.tanh(zs), zs)
        a_hi, a_lo = _split2(act)
        lora.append(_dot(jnp.concatenate([a_hi, a_lo, a_hi], axis=1), wl_ref[...]))

    chains = [(b, p, q) for q in range(n_sub) for b in batches for p in range(n_blocks)]
    ids = range(len(chains))
    sls = [slice(p * LANES, (p + 1) * LANES) for _, p, _ in chains]
    rws = [slice(q * L, (q + 1) * L) for _, _, q in chains]
    r = [xs[b][rws[i], p * LANES:(p + 1) * LANES] for i, (b, p, _) in enumerate(chains)]
    k = [xs[b][rws[i], width + p * LANES:width + (p + 1) * LANES] for i, (b, p, _) in enumerate(chains)]
    v = [xs[b][rws[i], 2 * width + p * LANES:2 * width + (p + 1) * LANES] for i, (b, p, _) in enumerate(chains)]
    lw, a = [], []
    for i, (b, p, _) in enumerate(chains):
        w_log = -_softplus(-(w0[:, sls[i]] + lora[b][rws[i], sls[i]])) - 0.5
        lw.append(-jnp.exp(w_log))
        a.append(_sigmoid(a0[:, sls[i]] + lora[b][rws[i], width + p * LANES:width + (p + 1) * LANES]))

    c = [_dot(cum3, jnp.concatenate(_split3(lw[i]), axis=0)) for i in ids]
    kk_raw = [k[i] * k_k[:, sls[i]] for i in ids]
    kk_ss = [head_sum(kk_raw[i] * kk_raw[i]) for i in ids]
    kmod = [k[i] * (1.0 + (a[i] - 1.0) * k_a[:, sls[i]]) for i in ids]
    bonus_dot = [head_sum(r[i] * kmod[i] * r_k[:, sls[i]]) for i in ids]

    lhs, rhs, xa_s, xr_b, v_s, upd, decay_last = [], [], [], [], [], [], []
    for i in ids:
        kk = kk_raw[i] * lax.rsqrt(jnp.maximum(kk_ss[i], 1e-24))
        bvec = kk * a[i]
        c_last = c[i][L - 1:L, :]
        g_in = jnp.exp(c[i])
        g_ex = jnp.exp(c[i] - lw[i])
        g_inv = jnp.exp(-c[i])
        g_rem = jnp.exp(c_last - c[i])
        xa = -kk * g_ex
        xr_b.append((r[i] * g_in).astype(BF16))
        xa_s.append(stacked(xa))
        v_s.append(stacked(v[i]))
        lhs.append(jnp.concatenate([xa.astype(BF16), xr_b[i]], axis=0))
        rhs.append(jnp.concatenate([stacked(bvec * g_inv), stacked(kmod[i] * g_inv)], axis=0))
        upd.append(jnp.concatenate([bvec * g_rem, kmod[i] * g_rem], axis=0).astype(BF16))
        decay_last.append(jnp.exp(c_last))

    gram = [_dot(lhs[i], rhs[i], NT) for i in ids]
    n_p = [jnp.where(strict, gram[i][:L, :2 * L], 0.0) for i in ids]
    a_ak = [jnp.where(strict, gram[i][:L, 2 * L:], 0.0).astype(BF16) for i in ids]
    a_rbk = [jnp.concatenate([jnp.where(incl, gram[i][L:, :2 * L], 0.0),
                              jnp.where(incl, gram[i][L:, 2 * L:], 0.0)], axis=1).astype(BF16) for i in ids]

    n_sq = L.bit_length() - 1
    qmat = [_dot(n_p[i].astype(BF16), stacked(n_p[i])) for i in ids]
    w1 = [_dot(a_ak[i], v_s[i]) for i in ids]
    pmat = [eye2 + n_p[i] for i in ids]
    for step in range(1, n_sq):
        q_s = [stacked(qmat[i]) for i in ids]
        if step < n_sq - 1:
            both = [_dot(jnp.concatenate([pmat[i], qmat[i]], axis=0).astype(BF16), q_s[i]) for i in ids]
            pmat = [pmat[i] + both[i][:L] for i in ids]
            qmat = [both[i][L:] for i in ids]
        else:
            pq = [_dot(pmat[i].astype(BF16), q_s[i]) for i in ids]
            pmat = [pmat[i] + pq[i] for i in ids]

    aw = [_dot(pmat[i].astype(BF16), jnp.concatenate([xa_s[i], stacked(w1[i])], axis=1)) for i in ids]
    n_state = n_batch * n_blocks
    h_t = [state_ref[j] for j in range(n_state)]
    y = []
    for q in range(n_sub):
        sub = range(q * n_state, (q + 1) * n_state)
        s9 = [_dot(jnp.concatenate([aw[i][:, :LANES].astype(BF16), xr_b[i]], axis=0),
                   h_t[i - q * n_state].astype(BF16), NT) for i in sub]
        u = [s9[j][:L] + aw[i][:, LANES:] for j, i in enumerate(sub)]
        y += [s9[j][L:] + _dot(a_rbk[i], jnp.concatenate([stacked(u[j]), v_s[i]], axis=0))
              for j, i in enumerate(sub)]
        for j, i in enumerate(sub):
            uv = jnp.concatenate([u[j], v[i]], axis=0).astype(BF16)
            h_t[j] = h_t[j] * decay_last[i] + jnp.where(same_head, _dot(uv, upd[i], TN), 0.0)
    for j in range(n_state):
        state_ref[j] = h_t[j]

    mean = [head_sum(y[i]) * (1.0 / HEAD_DIM) for i in ids]
    dlt = [y[i] - mean[i] for i in ids]
    var = [head_sum(dlt[i] * dlt[i]) * (1.0 / HEAD_DIM) for i in ids]
    for i, (b, p, _) in enumerate(chains):
        yn = dlt[i] * lax.rsqrt(var[i] + GN_EPS) * ln_w[:, sls[i]] + ln_b[:, sls[i]]
        gate = g_ref[b, rws[i], sls[i]]
        y_ref[b, rws[i], sls[i]] = ((yn + bonus_dot[i] * v[i]) * _silu(gate)).astype(y_ref.dtype)


def _rwkv(rw, zz, mu, w0, w2, a0, a2, k_k, k_a, r_k, ln_w, ln_b):
    b, s, w4 = rw.shape
    width = w4 // 4
    n_blocks = width // LANES
    L = RWKV_CHUNK
    mu_rkv = mu[:3 * width].reshape(1, 3 * width)
    mu_z = mu[3 * width:].reshape(1, DECAY_LORA + ICLR_LORA)
    w_lora = jnp.zeros((DECAY_LORA + ICLR_LORA, 2 * width), F32)
    w_lora = w_lora.at[:DECAY_LORA, :width].set(w2).at[DECAY_LORA:, width:].set(a2)
    wl_hi = w_lora.astype(BF16)
    wl_lo = (w_lora - wl_hi.astype(F32)).astype(BF16)
    wl_cat = jnp.concatenate([wl_hi, wl_hi, wl_lo], axis=0)
    vec = jnp.stack([k_k, k_a, r_k.reshape(width), ln_w, ln_b, w0, a0, jnp.zeros_like(w0)], axis=0)
    rows = L * RWKV_STEP_CHUNKS
    assert s % rows == 0
    return pl.pallas_call(
        functools.partial(_rwkv_kernel, n_blocks=n_blocks),
        grid=(s // rows,),
        in_specs=[
            pl.BlockSpec((b, rows, 3 * width), lambda j: (0, j, 0)),
            pl.BlockSpec((b, rows, LANES), lambda j: (0, j, 0)),
            pl.BlockSpec((b, rows, width), lambda j: (0, j, 3)),
            pl.BlockSpec((1, 3 * width), lambda j: (0, 0)),
            pl.BlockSpec((1, LANES), lambda j: (0, 0)),
            pl.BlockSpec((3 * LANES, 2 * width), lambda j: (0, 0)),
            pl.BlockSpec((8, width), lambda j: (0, 0)),
        ],
        out_specs=pl.BlockSpec((b, rows, width), lambda j: (0, j, 0)),
        out_shape=jax.ShapeDtypeStruct((b, s, width), BF16),
        scratch_shapes=[
            pltpu.VMEM((b * n_blocks, LANES, LANES), F32),
            pltpu.VMEM((b, 8, 3 * width), F32),
            pltpu.VMEM((b, 8, LANES), F32),
        ],
        compiler_params=pltpu.CompilerParams(
            dimension_semantics=("arbitrary",), vmem_limit_bytes=VMEM_LIMIT_BYTES),
        name="rwkv7",
    )(rw, zz, rw, mu_rkv, mu_z, wl_cat, vec)


def _sb_kernel(q_ref, k_ref, v_ref, sg_ref, o_ref, acc_ref, carry_ref, *, n_blocks):
    T = SB_BLOCK
    n_sub = q_ref.shape[1] // T
    step = pl.program_id(0)
    chains = [(h, b, slice(p * LANES, (p + 1) * LANES))
              for h in range(n_sub) for b in range(q_ref.shape[0]) for p in range(n_blocks)]
    ids = range(len(chains))
    lane = lax.broadcasted_iota(jnp.int32, (T, LANES), 1)
    head0 = lane < HEAD_DIM
    scale = HEAD_DIM ** -0.5
    qs = []
    for h, b, cols in chains:
        q = q_ref[b, h * T:(h + 1) * T, cols]
        zero = jnp.zeros_like(q)
        qs.append(jnp.concatenate([jnp.where(head0, q, zero), jnp.where(head0, zero, q)], axis=0) * scale)

    sr = lax.broadcasted_iota(jnp.int32, (T, T), 0)
    sc = lax.broadcasted_iota(jnp.int32, (T, T), 1)
    tri_ones = jnp.concatenate([jnp.where(sr > sc, 1.0, 0.0), jnp.ones((T, T), F32)], axis=1).astype(BF16)
    tri_ones2 = jnp.concatenate([tri_ones, tri_ones], axis=0)
    mr = lax.broadcasted_iota(jnp.int32, (2 * T, T), 0)
    mc = lax.broadcasted_iota(jnp.int32, (2 * T, T), 1)
    causal = mc < jnp.where(mr >= T, mr - T, mr)

    def tiles(sel, back, diag):
        starts = [pl.multiple_of((step * n_sub + h - back) * T, T) for h in range(n_sub)]
        z = {i: _dot(qs[i], k_ref[chains[i][1], pl.ds(starts[chains[i][0]], T), chains[i][2]], NT)
             for i in sel}
        soft = {i: jnp.log(1.0 + jnp.exp(-jnp.abs(z[i]))) for i in sel}
        log_1m = {i: -(jnp.maximum(z[i], 0.0) + soft[i]) for i in sel}
        if diag:
            log_1m = {i: jnp.where(causal, log_1m[i], 0.0) for i in sel}
        sums = {i: _dot(jnp.concatenate(_split2(log_1m[i]), axis=1), tri_ones2) for i in sel}
        for i in sel:
            h, b, cols = chains[i]
            log_b = jnp.minimum(z[i], 0.0) - soft[i]
            if diag:
                attn = jnp.where(causal, jnp.exp(log_b + sums[i][:, :T]), 0.0)
                acc_ref[i] = _dot(attn.astype(BF16), v_ref[b, pl.ds(starts[h], T), cols])
                carry_ref[i] = sums[i][:, T:]
            else:
                carry = carry_ref[i]
                attn = jnp.exp(log_b + sums[i][:, :T] + carry)
                acc_ref[i] += _dot(attn.astype(BF16), v_ref[b, pl.ds(starts[h], T), cols])
                carry_ref[i] = carry + sums[i][:, T:]

    def live():
        top = carry_ref[0]
        for i in ids[1:]:
            top = jnp.maximum(top, carry_ref[i])
        return (jnp.max(top) > -SB_F32_EXP_UNDERFLOW).astype(jnp.int32)

    tiles(ids, 0, True)
    n_common = step * n_sub

    def cond(state):
        it, alive = state
        return jnp.logical_and(it < n_common, alive > 0)

    def body(state):
        it, _ = state
        tiles(ids, it + 1, False)
        return it + 1, live()

    _, alive = lax.while_loop(cond, body, (jnp.int32(0), live()))

    for extra in range(1, n_sub):
        @pl.when(alive > 0)
        def _():
            tiles([i for i in ids if chains[i][0] >= extra], n_common + extra, False)

    head_ones = _head_ones()
    for i, (h, b, cols) in enumerate(chains):
        acc = acc_ref[i]
        o = jnp.where(head0, acc[:T], acc[T:])
        ms = _dot_exact_rhs(o * o, head_ones) * (1.0 / HEAD_DIM)
        o_ref[b, h * T:(h + 1) * T, cols] = o * lax.rsqrt(ms + NORM_EPS) * sg_ref[:, cols]


def _stick_breaking(qkv, sb_g):
    b, s, w3 = qkv.shape
    width = w3 // 3
    n_blocks = width // LANES
    T = SB_BLOCK
    resident = pl.Buffered(1)
    rows = T * SB_STEP_BLOCKS
    assert s % rows == 0
    n_chains = SB_STEP_BLOCKS * b * n_blocks
    return pl.pallas_call(
        functools.partial(_sb_kernel, n_blocks=n_blocks),
        grid=(s // rows,),
        in_specs=[
            pl.BlockSpec((b, rows, width), lambda j: (0, j, 0)),
            pl.BlockSpec((b, s, width), lambda j: (0, 0, 1), pipeline_mode=resident),
            pl.BlockSpec((b, s, width), lambda j: (0, 0, 2), pipeline_mode=resident),
            pl.BlockSpec((1, width), lambda j: (0, 0)),
        ],
        out_specs=pl.BlockSpec((b, rows, width), lambda j: (0, j, 0)),
        out_shape=jax.ShapeDtypeStruct((b, s, width), F32),
        scratch_shapes=[
            pltpu.VMEM((n_chains, 2 * T, LANES), F32),
            pltpu.VMEM((n_chains, 2 * T, T), F32),
        ],
        compiler_params=pltpu.CompilerParams(
            dimension_semantics=("arbitrary",), vmem_limit_bytes=VMEM_LIMIT_BYTES),
        name="stick_breaking",
    )(qkv, qkv, qkv, sb_g.reshape(1, width))


def _out_proj_kernel(yr_ref, os_ref, sg_ref, w_ref, x_ref, gate_ref, fg_ref, o_ref, perm_ref, *, final_norm):
    half = yr_ref.shape[-1]
    n_qb, sub = os_ref.shape[1], os_ref.shape[2]
    o_tok = os_ref[0].reshape(n_qb * sub, half)
    cols = []
    for cb in range(half // LANES):
        perm_ref[cb] = o_tok[:, cb * LANES:(cb + 1) * LANES]
        cols.append(jnp.concatenate(
            [perm_ref[cb, pl.ds(ii, n_qb, stride=sub), :] for ii in range(sub)], axis=0))
    o_rows = jnp.concatenate(cols, axis=1)
    y_sb = (o_rows * _silu(sg_ref[0])).astype(BF16)
    y = _dot(yr_ref[0], w_ref[:half, :]) + _dot(y_sb, w_ref[half:, :])
    x = x_ref[0] + gate_ref[0] * y
    if final_norm:
        ms = jnp.mean(x * x, axis=-1, keepdims=True)
        x = x * lax.rsqrt(ms + NORM_EPS) * fg_ref[...]
    o_ref[0] = x


def _out_proj(y_rw, o_sb, sb_gate, w_bf16, x, gate, final_g, final_norm):
    b, s, d = x.shape
    half = y_rw.shape[-1]
    n_qb = s // SB_BLOCK
    sub = 8
    tm = sub * n_qb
    o_sb4 = o_sb.reshape(b, n_qb, SB_BLOCK, half)
    return pl.pallas_call(
        functools.partial(_out_proj_kernel, final_norm=final_norm),
        grid=(b, s // tm),
        in_specs=[
            pl.BlockSpec((1, tm, half), lambda i, j: (i, j, 0)),
            pl.BlockSpec((1, n_qb, sub, half), lambda i, j: (i, 0, j, 0)),
            pl.BlockSpec((1, tm, half), lambda i, j: (i, j, 0)),
            pl.BlockSpec((2 * half, d), lambda i, j: (0, 0)),
            pl.BlockSpec((1, tm, d), lambda i, j: (i, j, 0)),
            pl.BlockSpec((1, 1, d), lambda i, j: (i, 0, 0)),
            pl.BlockSpec((1, d), lambda i, j: (0, 0)),
        ],
        out_specs=pl.BlockSpec((1, tm, d), lambda i, j: (i, j, 0)),
        out_shape=jax.ShapeDtypeStruct((b, s, d), F32),
        scratch_shapes=[pltpu.VMEM((half // LANES, tm, LANES), F32)],
        compiler_params=pltpu.CompilerParams(
            dimension_semantics=("parallel", "parallel"), vmem_limit_bytes=VMEM_LIMIT_BYTES),
        name="out_proj",
    )(y_rw, o_sb4, sb_gate, w_bf16, x, gate.reshape(b, 1, d), final_g.reshape(1, d))


def _row_tile(s):
    for tm in (512, 256, 128, 64, 32, 16, 8):
        if s % tm == 0:
            return tm
    raise ValueError(f"sequence length {s} must be a multiple of 8")


def kernel(x, c, norm_g, ada_w, ada_b, w_in, w_out, tshift_mu, decay_w0, decay_w2, iclr_a0, iclr_a2,
           k_k, k_a, r_k, rwkv_ln_w, rwkv_ln_b, sb_norm_g, final_g):
    depth = norm_g.shape[0]
    b, s, d = x.shape
    assert s % SB_BLOCK == 0 and s % RWKV_CHUNK == 0
    tm = _row_tile(s)
    mod = _ada_mod(c, ada_w, ada_b)
    w_in_b = w_in.astype(BF16)
    w_out_b = w_out.astype(BF16)
    for l in range(depth):
        shift, scale, gate = mod[l, :, :d], mod[l, :, d:2 * d], mod[l, :, 2 * d:]
        rw, zz, sb_qkv, sb_gate = _in_proj(x, norm_g[l], scale, shift, w_in_b[l], tm)
        y_rw = _rwkv(rw, zz, tshift_mu[l], decay_w0[l], decay_w2[l], iclr_a0[l], iclr_a2[l],
                     k_k[l], k_a[l], r_k[l], rwkv_ln_w[l], rwkv_ln_b[l])
        o_sb = _stick_breaking(sb_qkv, sb_norm_g[l])
        x = _out_proj(y_rw, o_sb, sb_gate, w_out_b[l], x, gate, final_g, l == depth - 1)
    return x
```

```python
import functools

import jax
import jax.numpy as jnp
from jax import lax
from jax.experimental import pallas as pl
from jax.experimental.pallas import tpu as pltpu

F32 = jnp.float32
BF16 = jnp.bfloat16

HEAD_DIM = 64
LANES = 128
HEADS_PER_BLOCK = LANES // HEAD_DIM
RWKV_CHUNK = 64
RWKV_STEP_CHUNKS = 4
SB_BLOCK = 128
SB_STEP_BLOCKS = 2
SB_F32_EXP_UNDERFLOW = 106.0
DECAY_LORA = 64
ICLR_LORA = 64
NORM_EPS = 1e-6
GN_EPS = 64e-5
VMEM_LIMIT_BYTES = 56 * 1024 * 1024

NN = (((1,), (0,)), ((), ()))
NT = (((1,), (1,)), ((), ()))
TN = (((0,), (0,)), ((), ()))


def _dot(a, b, dims=NN):
    return lax.dot_general(a, b, dims, preferred_element_type=F32)


def _split2(x):
    hi = x.astype(BF16)
    lo = (x - hi.astype(F32)).astype(BF16)
    return hi, lo


def _split3(x):
    hi = x.astype(BF16)
    r1 = x - hi.astype(F32)
    mid = r1.astype(BF16)
    lo = (r1 - mid.astype(F32)).astype(BF16)
    return hi, mid, lo


def _dot_exact_rhs(x, rhs_bf16, dims=NN):
    hi, mid, lo = _split3(x)
    return _dot(hi, rhs_bf16, dims) + _dot(mid, rhs_bf16, dims) + _dot(lo, rhs_bf16, dims)


def _dot_x3(a, b_hi, b_lo):
    a_hi, a_lo = _split2(a)
    return _dot(a_hi, b_hi) + _dot(a_lo, b_hi) + _dot(a_hi, b_lo)


def _softplus(u):
    return jnp.maximum(u, 0.0) + jnp.log(1.0 + jnp.exp(-jnp.abs(u)))


def _sigmoid(u):
    return 1.0 / (1.0 + jnp.exp(-u))


def _silu(u):
    return u * _sigmoid(u)


def _head_ones():
    r = lax.broadcasted_iota(jnp.int32, (LANES, LANES), 0)
    c = lax.broadcasted_iota(jnp.int32, (LANES, LANES), 1)
    return jnp.where((r >= HEAD_DIM) == (c >= HEAD_DIM), 1.0, 0.0).astype(BF16)


def _ada_kernel(c_ref, w_ref, b_ref, o_ref):
    c_act = _silu(c_ref[...])
    w_hi, w_lo = _split2(w_ref[0])
    o_ref[0] = _dot_x3(c_act, w_hi, w_lo) + b_ref[0]


def _ada_mod(c, ada_w, ada_b):
    depth, d, d3 = ada_w.shape
    b = c.shape[0]
    rows = 8
    c8 = jnp.zeros((rows, d), F32).at[:b].set(c)
    out = pl.pallas_call(
        _ada_kernel,
        grid=(depth, d3 // d),
        in_specs=[
            pl.BlockSpec((rows, d), lambda l, j: (0, 0)),
            pl.BlockSpec((1, d, d), lambda l, j: (l, 0, j)),
            pl.BlockSpec((1, 1, d), lambda l, j: (l, 0, j)),
        ],
        out_specs=pl.BlockSpec((1, rows, d), lambda l, j: (l, 0, j)),
        out_shape=jax.ShapeDtypeStruct((depth, rows, d3), F32),
        name="ada_mod",
    )(c8, ada_w, ada_b.reshape(depth, 1, d3))
    return out[:, :b]


def _in_proj_kernel(x_ref, g_ref, sc_ref, sh_ref, w_ref, rw_ref, zz_ref, sbq_ref, sbg_ref, *, widths):
    rw_w, zz_w, sbq_w, sbg_w = widths
    x = x_ref[0]
    ms = jnp.mean(x * x, axis=-1, keepdims=True)
    h = x * lax.rsqrt(ms + NORM_EPS) * g_ref[...]
    h = h * (1.0 + sc_ref[0]) + sh_ref[0]
    hb = h.astype(BF16)

    def emit(out_ref, out_col, w_col, width):
        step = 512
        for c0 in range(0, width, step):
            cw = min(step, width - c0)
            res = _dot(hb, w_ref[:, w_col + c0:w_col + c0 + cw])
            out_ref[0, :, out_col + c0:out_col + c0 + cw] = res.astype(out_ref.dtype)

    shift_rkv = rw_w - sbg_w
    emit(rw_ref, 0, 0, shift_rkv)
    emit(zz_ref, 0, shift_rkv, zz_w)
    emit(rw_ref, shift_rkv, shift_rkv + zz_w, sbg_w)
    sb0 = rw_w + zz_w
    emit(sbq_ref, 0, sb0, sbq_w)
    emit(sbg_ref, 0, sb0 + sbq_w, sbg_w)


def _in_proj(x, g, scale, shift, w_bf16, tm):
    b, s, d = x.shape
    width = d // 2
    widths = (4 * width, DECAY_LORA + ICLR_LORA, 3 * width, width)
    n_cols = w_bf16.shape[1]
    assert sum(widths) == n_cols
    return pl.pallas_call(
        functools.partial(_in_proj_kernel, widths=widths),
        grid=(b, s // tm),
        in_specs=[
            pl.BlockSpec((1, tm, d), lambda i, j: (i, j, 0)),
            pl.BlockSpec((1, d), lambda i, j: (0, 0)),
            pl.BlockSpec((1, 1, d), lambda i, j: (i, 0, 0)),
            pl.BlockSpec((1, 1, d), lambda i, j: (i, 0, 0)),
            pl.BlockSpec((d, n_cols), lambda i, j: (0, 0)),
        ],
        out_specs=[
            pl.BlockSpec((1, tm, widths[0]), lambda i, j: (i, j, 0)),
            pl.BlockSpec((1, tm, widths[1]), lambda i, j: (i, j, 0)),
            pl.BlockSpec((1, tm, widths[2]), lambda i, j: (i, j, 0)),
            pl.BlockSpec((1, tm, widths[3]), lambda i, j: (i, j, 0)),
        ],
        out_shape=[
            jax.ShapeDtypeStruct((b, s, widths[0]), F32),
            jax.ShapeDtypeStruct((b, s, widths[1]), F32),
            jax.ShapeDtypeStruct((b, s, widths[2]), BF16),
            jax.ShapeDtypeStruct((b, s, widths[3]), F32),
        ],
        compiler_params=pltpu.CompilerParams(
            dimension_semantics=("parallel", "parallel"), vmem_limit_bytes=VMEM_LIMIT_BYTES),
        name="in_proj",
    )(x, g.reshape(1, d), scale.reshape(b, 1, d), shift.reshape(b, 1, d), w_bf16)


def _rwkv_kernel(rkv_ref, zz_ref, g_ref, mu_ref, muz_ref, wl_ref, vec_ref, y_ref,
                 state_ref, prev_ref, prevz_ref, *, n_blocks):
    L = RWKV_CHUNK
    n_batch, rows = rkv_ref.shape[0], rkv_ref.shape[1]
    n_sub = rows // L
    width = n_blocks * LANES
    t = pl.program_id(0)

    @pl.when(t == 0)
    def _():
        state_ref[...] = jnp.zeros_like(state_ref)
        prev_ref[...] = jnp.zeros_like(prev_ref)
        prevz_ref[...] = jnp.zeros_like(prevz_ref)

    def token_shift(x, carry_ref, b, mu):
        row = lax.broadcasted_iota(jnp.int32, x.shape, 0)
        prev = jnp.where(row == 0, carry_ref[b, 7:8, :], pltpu.roll(x, 1, 0))
        carry_ref[b] = x[rows - 8:rows]
        return x + (prev - x) * mu

    k_k = vec_ref[0:1, :]
    k_a = vec_ref[1:2, :]
    r_k = vec_ref[2:3, :]
    ln_w = vec_ref[3:4, :]
    ln_b = vec_ref[4:5, :]
    w0 = vec_ref[5:6, :]
    a0 = vec_ref[6:7, :]

    lane = lax.broadcasted_iota(jnp.int32, (L, LANES), 1)
    head0 = lane < HEAD_DIM
    tr = lax.broadcasted_iota(jnp.int32, (L, 3 * L), 0)
    tc = lax.broadcasted_iota(jnp.int32, (L, 3 * L), 1)
    tc = jnp.where(tc >= 2 * L, tc - 2 * L, jnp.where(tc >= L, tc - L, tc))
    cum3 = jnp.where(tc <= tr, 1.0, 0.0).astype(BF16)
    pr = lax.broadcasted_iota(jnp.int32, (L, 2 * L), 0)
    pc = lax.broadcasted_iota(jnp.int32, (L, 2 * L), 1)
    pc = jnp.where(pc >= L, pc - L, pc)
    strict = pc < pr
    incl = pc <= pr
    eye2 = jnp.where(pc == pr, 1.0, 0.0)
    br = lax.broadcasted_iota(jnp.int32, (LANES, LANES), 0)
    bc = lax.broadcasted_iota(jnp.int32, (LANES, LANES), 1)
    same_head = (br >= HEAD_DIM) == (bc >= HEAD_DIM)

    def stacked(x):
        xb = x.astype(BF16)
        zero = jnp.zeros_like(xb)
        return jnp.concatenate([jnp.where(head0, xb, zero), jnp.where(head0, zero, xb)], axis=0)

    def head_sum(x):
        s0 = jnp.sum(jnp.where(head0, x, 0.0), axis=1, keepdims=True)
        s1 = jnp.sum(jnp.where(head0, 0.0, x), axis=1, keepdims=True)
        return jnp.where(head0, s0, s1)

    batches = range(n_batch)
    lane_z = lax.broadcasted_iota(jnp.int32, (rows, LANES), 1)
    xs, lora = [], []
    for b in batches:
        xs.append(token_shift(rkv_ref[b], prev_ref, b, mu_ref[...]))
        zs = token_shift(zz_ref[b], prevz_ref, b, muz_ref[...])
        act = jnp.where(lane_z < DECAY_LORA, jnp.tanh(zs), zs)
        a_hi, a_lo = _split2(act)
        lora.append(_dot(jnp.concatenate([a_hi, a_lo, a_hi], axis=1), wl_ref[...]))

    chains = [(b, p, q) for q in range(n_sub) for b in batches for p in range(n_blocks)]
    ids = range(len(chains))
    sls = [slice(p * LANES, (p + 1) * LANES) for _, p, _ in chains]
    rws = [slice(q * L, (q + 1) * L) for _, _, q in chains]
    r = [xs[b][rws[i], p * LANES:(p + 1) * LANES] for i, (b, p, _) in enumerate(chains)]
    k = [xs[b][rws[i], width + p * LANES:width + (p + 1) * LANES] for i, (b, p, _) in enumerate(chains)]
    v = [xs[b][rws[i], 2 * width + p * LANES:2 * width + (p + 1) * LANES] for i, (b, p, _) in enumerate(chains)]
    lw, a = [], []
    for i, (b, p, _) in enumerate(chains):
        w_log = -_softplus(-(w0[:, sls[i]] + lora[b][rws[i], sls[i]])) - 0.5
        lw.append(-jnp.exp(w_log))
        a.append(_sigmoid(a0[:, sls[i]] + lora[b][rws[i], width + p * LANES:width + (p + 1) * LANES]))

    c = [_dot(cum3, jnp.concatenate(_split3(lw[i]), axis=0)) for i in ids]
    kk_raw = [k[i] * k_k[:, sls[i]] for i in ids]
    kk_ss = [head_sum(kk_raw[i] * kk_raw[i]) for i in ids]
    kmod = [k[i] * (1.0 + (a[i] - 1.0) * k_a[:, sls[i]]) for i in ids]
    bonus_dot = [head_sum(r[i] * kmod[i] * r_k[:, sls[i]]) for i in ids]

    lhs, rhs, xa_s, xr_b, v_s, upd, decay_last = [], [], [], [], [], [], []
    for i in ids:
        kk = kk_raw[i] * lax.rsqrt(jnp.maximum(kk_ss[i], 1e-24))
        bvec = kk * a[i]
        c_last = c[i][L - 1:L, :]
        g_in = jnp.exp(c[i])
        g_ex = jnp.exp(c[i] - lw[i])
        g_inv = jnp.exp(-c[i])
        g_rem = jnp.exp(c_last - c[i])
        xa = -kk * g_ex
        xr_b.append((r[i] * g_in).astype(BF16))
        xa_s.append(stacked(xa))
        v_s.append(stacked(v[i]))
        lhs.append(jnp.concatenate([xa.astype(BF16), xr_b[i]], axis=0))
        rhs.append(jnp.concatenate([stacked(bvec * g_inv), stacked(kmod[i] * g_inv)], axis=0))
        upd.append(jnp.concatenate([bvec * g_rem, kmod[i] * g_rem], axis=0).astype(BF16))
        decay_last.append(jnp.exp(c_last))

    gram = [_dot(lhs[i], rhs[i], NT) for i in ids]
    n_p = [jnp.where(strict, gram[i][:L, :2 * L], 0.0) for i in ids]
    a_ak = [jnp.where(strict, gram[i][:L, 2 * L:], 0.0).astype(BF16) for i in ids]
    a_rbk = [jnp.concatenate([jnp.where(incl, gram[i][L:, :2 * L], 0.0),
                              jnp.where(incl, gram[i][L:, 2 * L:], 0.0)], axis=1).astype(BF16) for i in ids]

    n_sq = L.bit_length() - 1
    qmat = [_dot(n_p[i].astype(BF16), stacked(n_p[i])) for i in ids]
    w1 = [_dot(a_ak[i], v_s[i]) for i in ids]
    pmat = [eye2 + n_p[i] for i in ids]
    for step in range(1, n_sq):
        q_s = [stacked(qmat[i]) for i in ids]
        if step < n_sq - 1:
            both = [_dot(jnp.concatenate([pmat[i], qmat[i]], axis=0).astype(BF16), q_s[i]) for i in ids]
            pmat = [pmat[i] + both[i][:L] for i in ids]
            qmat = [both[i][L:] for i in ids]
        else:
            pq = [_dot(pmat[i].astype(BF16), q_s[i]) for i in ids]
            pmat = [pmat[i] + pq[i] for i in ids]

    aw = [_dot(pmat[i].astype(BF16), jnp.concatenate([xa_s[i], stacked(w1[i])], axis=1)) for i in ids]
    n_state = n_batch * n_blocks
    h_t = [state_ref[j] for j in range(n_state)]
    y = []
    for q in range(n_sub):
        sub = range(q * n_state, (q + 1) * n_state)
        s9 = [_dot(jnp.concatenate([aw[i][:, :LANES].astype(BF16), xr_b[i]], axis=0),
                   h_t[i - q * n_state].astype(BF16), NT) for i in sub]
        u = [s9[j][:L] + aw[i][:, LANES:] for j, i in enumerate(sub)]
        y += [s9[j][L:] + _dot(a_rbk[i], jnp.concatenate([stacked(u[j]), v_s[i]], axis=0))
              for j, i in enumerate(sub)]
        for j, i in enumerate(sub):
            uv = jnp.concatenate([u[j], v[i]], axis=0).astype(BF16)
            h_t[j] = h_t[j] * decay_last[i] + jnp.where(same_head, _dot(uv, upd[i], TN), 0.0)
    for j in range(n_state):
        state_ref[j] = h_t[j]

    mean = [head_sum(y[i]) * (1.0 / HEAD_DIM) for i in ids]
    dlt = [y[i] - mean[i] for i in ids]
    var = [head_sum(dlt[i] * dlt[i]) * (1.0 / HEAD_DIM) for i in ids]
    for i, (b, p, _) in enumerate(chains):
        yn = dlt[i] * lax.rsqrt(var[i] + GN_EPS) * ln_w[:, sls[i]] + ln_b[:, sls[i]]
        gate = g_ref[b, rws[i], sls[i]]
        y_ref[b, rws[i], sls[i]] = ((yn + bonus_dot[i] * v[i]) * _silu(gate)).astype(y_ref.dtype)


def _rwkv(rw, zz, mu, w0, w2, a0, a2, k_k, k_a, r_k, ln_w, ln_b):
    b, s, w4 = rw.shape
    width = w4 // 4
    n_blocks = width // LANES
    L = RWKV_CHUNK
    mu_rkv = mu[:3 * width].reshape(1, 3 * width)
    mu_z = mu[3 * width:].reshape(1, DECAY_LORA + ICLR_LORA)
    w_lora = jnp.zeros((DECAY_LORA + ICLR_LORA, 2 * width), F32)
    w_lora = w_lora.at[:DECAY_LORA, :width].set(w2).at[DECAY_LORA:, width:].set(a2)
    wl_hi = w_lora.astype(BF16)
    wl_lo = (w_lora - wl_hi.astype(F32)).astype(BF16)
    wl_cat = jnp.concatenate([wl_hi, wl_hi, wl_lo], axis=0)
    vec = jnp.stack([k_k, k_a, r_k.reshape(width), ln_w, ln_b, w0, a0, jnp.zeros_like(w0)], axis=0)
    rows = L * RWKV_STEP_CHUNKS
    assert s % rows == 0
    return pl.pallas_call(
        functools.partial(_rwkv_kernel, n_blocks=n_blocks),
        grid=(s // rows,),
        in_specs=[
            pl.BlockSpec((b, rows, 3 * width), lambda j: (0, j, 0)),
            pl.BlockSpec((b, rows, LANES), lambda j: (0, j, 0)),
            pl.BlockSpec((b, rows, width), lambda j: (0, j, 3)),
            pl.BlockSpec((1, 3 * width), lambda j: (0, 0)),
            pl.BlockSpec((1, LANES), lambda j: (0, 0)),
            pl.BlockSpec((3 * LANES, 2 * width), lambda j: (0, 0)),
            pl.BlockSpec((8, width), lambda j: (0, 0)),
        ],
        out_specs=pl.BlockSpec((b, rows, width), lambda j: (0, j, 0)),
        out_shape=jax.ShapeDtypeStruct((b, s, width), BF16),
        scratch_shapes=[
            pltpu.VMEM((b * n_blocks, LANES, LANES), F32),
            pltpu.VMEM((b, 8, 3 * width), F32),
            pltpu.VMEM((b, 8, LANES), F32),
        ],
        compiler_params=pltpu.CompilerParams(
            dimension_semantics=("arbitrary",), vmem_limit_bytes=VMEM_LIMIT_BYTES),
        name="rwkv7",
    )(rw, zz, rw, mu_rkv, mu_z, wl_cat, vec)


def _sb_kernel(q_ref, k_ref, v_ref, sg_ref, o_ref, acc_ref, carry_ref, *, n_blocks):
    T = SB_BLOCK
    n_sub = q_ref.shape[1] // T
    step = pl.program_id(0)
    chains = [(h, b, slice(p * LANES, (p + 1) * LANES))
              for h in range(n_sub) for b in range(q_ref.shape[0]) for p in range(n_blocks)]
    ids = range(len(chains))
    lane = lax.broadcasted_iota(jnp.int32, (T, LANES), 1)
    head0 = lane < HEAD_DIM
    scale = HEAD_DIM ** -0.5
    qs = []
    for h, b, cols in chains:
        q = q_ref[b, h * T:(h + 1) * T, cols]
        zero = jnp.zeros_like(q)
        qs.append(jnp.concatenate([jnp.where(head0, q, zero), jnp.where(head0, zero, q)], axis=0) * scale)

    sr = lax.broadcasted_iota(jnp.int32, (T, T), 0)
    sc = lax.broadcasted_iota(jnp.int32, (T, T), 1)
    tri_ones = jnp.concatenate([jnp.where(sr > sc, 1.0, 0.0), jnp.ones((T, T), F32)], axis=1).astype(BF16)
    tri_ones2 = jnp.concatenate([tri_ones, tri_ones], axis=0)
    mr = lax.broadcasted_iota(jnp.int32, (2 * T, T), 0)
    mc = lax.broadcasted_iota(jnp.int32, (2 * T, T), 1)
    causal = mc < jnp.where(mr >= T, mr - T, mr)

    def tiles(sel, back, diag):
        starts = [pl.multiple_of((step * n_sub + h - back) * T, T) for h in range(n_sub)]
        z = {i: _dot(qs[i], k_ref[chains[i][1], pl.ds(starts[chains[i][0]], T), chains[i][2]], NT)
             for i in sel}
        neg_1m = {i: jnp.maximum(z[i], 0.0) + jnp.log(1.0 + jnp.exp(-jnp.abs(z[i]))) for i in sel}
        masked = {i: jnp.where(causal, neg_1m[i], 0.0) for i in sel} if diag else neg_1m
        sums = {i: _dot(jnp.concatenate(_split2(masked[i]), axis=1), tri_ones2) for i in sel}
        for i in sel:
            h, b, cols = chains[i]
            log_b = z[i] - neg_1m[i]
            if diag:
                attn = jnp.where(causal, jnp.exp(log_b - sums[i][:, :T]), 0.0)
                acc_ref[i] = _dot(attn.astype(BF16), v_ref[b, pl.ds(starts[h], T), cols])
                carry_ref[i] = sums[i][:, T:]
            else:
                carry = carry_ref[i]
                attn = jnp.exp(log_b - sums[i][:, :T] - carry)
                acc_ref[i] += _dot(attn.astype(BF16), v_ref[b, pl.ds(starts[h], T), cols])
                carry_ref[i] = carry + sums[i][:, T:]

    def live():
        low = carry_ref[0]
        for i in ids[1:]:
            low = jnp.minimum(low, carry_ref[i])
        return (jnp.min(low) < SB_F32_EXP_UNDERFLOW).astype(jnp.int32)

    tiles(ids, 0, True)
    n_common = step * n_sub

    def cond(state):
        it, alive = state
        return jnp.logical_and(it < n_common, alive > 0)

    def body(state):
        it, _ = state
        tiles(ids, it + 1, False)
        return it + 1, live()

    _, alive = lax.while_loop(cond, body, (jnp.int32(0), live()))

    for extra in range(1, n_sub):
        @pl.when(alive > 0)
        def _():
            tiles([i for i in ids if chains[i][0] >= extra], n_common + extra, False)

    head_ones = _head_ones()
    for i, (h, b, cols) in enumerate(chains):
        acc = acc_ref[i]
        o = jnp.where(head0, acc[:T], acc[T:])
        ms = _dot_exact_rhs(o * o, head_ones) * (1.0 / HEAD_DIM)
        o_ref[b, h * T:(h + 1) * T, cols] = o * lax.rsqrt(ms + NORM_EPS) * sg_ref[:, cols]


def _stick_breaking(qkv, sb_g):
    b, s, w3 = qkv.shape
    width = w3 // 3
    n_blocks = width // LANES
    T = SB_BLOCK
    resident = pl.Buffered(1)
    rows = T * SB_STEP_BLOCKS
    assert s % rows == 0
    n_chains = SB_STEP_BLOCKS * b * n_blocks
    return pl.pallas_call(
        functools.partial(_sb_kernel, n_blocks=n_blocks),
        grid=(s // rows,),
        in_specs=[
            pl.BlockSpec((b, rows, width), lambda j: (0, j, 0)),
            pl.BlockSpec((b, s, width), lambda j: (0, 0, 1), pipeline_mode=resident),
            pl.BlockSpec((b, s, width), lambda j: (0, 0, 2), pipeline_mode=resident),
            pl.BlockSpec((1, width), lambda j: (0, 0)),
        ],
        out_specs=pl.BlockSpec((b, rows, width), lambda j: (0, j, 0)),
        out_shape=jax.ShapeDtypeStruct((b, s, width), F32),
        scratch_shapes=[
            pltpu.VMEM((n_chains, 2 * T, LANES), F32),
            pltpu.VMEM((n_chains, 2 * T, T), F32),
        ],
        compiler_params=pltpu.CompilerParams(
            dimension_semantics=("arbitrary",), vmem_limit_bytes=VMEM_LIMIT_BYTES),
        name="stick_breaking",
    )(qkv, qkv, qkv, sb_g.reshape(1, width))


def _out_proj_kernel(yr_ref, os_ref, sg_ref, w_ref, x_ref, gate_ref, fg_ref, o_ref, perm_ref, *, final_norm):
    half = yr_ref.shape[-1]
    n_qb, sub = os_ref.shape[1], os_ref.shape[2]
    o_tok = os_ref[0].reshape(n_qb * sub, half)
    cols = []
    for cb in range(half // LANES):
        perm_ref[cb] = o_tok[:, cb * LANES:(cb + 1) * LANES]
        cols.append(jnp.concatenate(
            [perm_ref[cb, pl.ds(ii, n_qb, stride=sub), :] for ii in range(sub)], axis=0))
    o_rows = jnp.concatenate(cols, axis=1)
    y_sb = (o_rows * _silu(sg_ref[0])).astype(BF16)
    y = _dot(yr_ref[0], w_ref[:half, :]) + _dot(y_sb, w_ref[half:, :])
    x = x_ref[0] + gate_ref[0] * y
    if final_norm:
        ms = jnp.mean(x * x, axis=-1, keepdims=True)
        x = x * lax.rsqrt(ms + NORM_EPS) * fg_ref[...]
    o_ref[0] = x


def _out_proj(y_rw, o_sb, sb_gate, w_bf16, x, gate, final_g, final_norm):
    b, s, d = x.shape
    half = y_rw.shape[-1]
    n_qb = s // SB_BLOCK
    sub = 8
    tm = sub * n_qb
    o_sb4 = o_sb.reshape(b, n_qb, SB_BLOCK, half)
    return pl.pallas_call(
        functools.partial(_out_proj_kernel, final_norm=final_norm),
        grid=(b, s // tm),
        in_specs=[
            pl.BlockSpec((1, tm, half), lambda i, j: (i, j, 0)),
            pl.BlockSpec((1, n_qb, sub, half), lambda i, j: (i, 0, j, 0)),
            pl.BlockSpec((1, tm, half), lambda i, j: (i, j, 0)),
            pl.BlockSpec((2 * half, d), lambda i, j: (0, 0)),
            pl.BlockSpec((1, tm, d), lambda i, j: (i, j, 0)),
            pl.BlockSpec((1, 1, d), lambda i, j: (i, 0, 0)),
            pl.BlockSpec((1, d), lambda i, j: (0, 0)),
        ],
        out_specs=pl.BlockSpec((1, tm, d), lambda i, j: (i, j, 0)),
        out_shape=jax.ShapeDtypeStruct((b, s, d), F32),
        scratch_shapes=[pltpu.VMEM((half // LANES, tm, LANES), F32)],
        compiler_params=pltpu.CompilerParams(
            dimension_semantics=("parallel", "parallel"), vmem_limit_bytes=VMEM_LIMIT_BYTES),
        name="out_proj",
    )(y_rw, o_sb4, sb_gate, w_bf16, x, gate.reshape(b, 1, d), final_g.reshape(1, d))


def _row_tile(s):
    for tm in (512, 256, 128, 64, 32, 16, 8):
        if s % tm == 0:
            return tm
    raise ValueError(f"sequence length {s} must be a multiple of 8")


def kernel(x, c, norm_g, ada_w, ada_b, w_in, w_out, tshift_mu, decay_w0, decay_w2, iclr_a0, iclr_a2,
           k_k, k_a, r_k, rwkv_ln_w, rwkv_ln_b, sb_norm_g, final_g):
    depth = norm_g.shape[0]
    b, s, d = x.shape
    assert s % SB_BLOCK == 0 and s % RWKV_CHUNK == 0
    tm = _row_tile(s)
    mod = _ada_mod(c, ada_w, ada_b)
    w_in_b = w_in.astype(BF16)
    w_out_b = w_out.astype(BF16)
    for l in range(depth):
        shift, scale, gate = mod[l, :, :d], mod[l, :, d:2 * d], mod[l, :, 2 * d:]
        rw, zz, sb_qkv, sb_gate = _in_proj(x, norm_g[l], scale, shift, w_in_b[l], tm)
        y_rw = _rwkv(rw, zz, tshift_mu[l], decay_w0[l], decay_w2[l], iclr_a0[l], iclr_a2[l],
                     k_k[l], k_a[l], r_k[l], rwkv_ln_w[l], rwkv_ln_b[l])
        o_sb = _stick_breaking(sb_qkv, sb_norm_g[l])
        x = _out_proj(y_rw, o_sb, sb_gate, w_out_b[l], x, gate, final_g, l == depth - 1)
    return x
```

```python
import functools

import jax
import jax.numpy as jnp
from jax import lax
from jax.experimental import pallas as pl
from jax.experimental.pallas import tpu as pltpu

F32 = jnp.float32
BF16 = jnp.bfloat16

HEAD_DIM = 64
LANES = 128
HEADS_PER_BLOCK = LANES // HEAD_DIM
RWKV_CHUNK = 64
RWKV_STEP_CHUNKS = 4
SB_BLOCK = 128
SB_STEP_BLOCKS = 2
SB_F32_EXP_UNDERFLOW = 106.0
DECAY_SCALE = 0.6065306597126334
DECAY_LORA = 64
ICLR_LORA = 64
NORM_EPS = 1e-6
GN_EPS = 64e-5
VMEM_LIMIT_BYTES = 56 * 1024 * 1024

NN = (((1,), (0,)), ((), ()))
NT = (((1,), (1,)), ((), ()))
TN = (((0,), (0,)), ((), ()))


def _dot(a, b, dims=NN):
    return lax.dot_general(a, b, dims, preferred_element_type=F32)


def _split2(x):
    hi = x.astype(BF16)
    lo = (x - hi.astype(F32)).astype(BF16)
    return hi, lo


def _split3(x):
    hi = x.astype(BF16)
    r1 = x - hi.astype(F32)
    mid = r1.astype(BF16)
    lo = (r1 - mid.astype(F32)).astype(BF16)
    return hi, mid, lo


def _dot_exact_rhs(x, rhs_bf16, dims=NN):
    hi, mid, lo = _split3(x)
    return _dot(hi, rhs_bf16, dims) + _dot(mid, rhs_bf16, dims) + _dot(lo, rhs_bf16, dims)


def _dot_x3(a, b_hi, b_lo):
    a_hi, a_lo = _split2(a)
    return _dot(a_hi, b_hi) + _dot(a_lo, b_hi) + _dot(a_hi, b_lo)


def _sigmoid(u):
    return 1.0 / (1.0 + jnp.exp(-u))


def _silu(u):
    return u * _sigmoid(u)


def _head_ones():
    r = lax.broadcasted_iota(jnp.int32, (LANES, LANES), 0)
    c = lax.broadcasted_iota(jnp.int32, (LANES, LANES), 1)
    return jnp.where((r >= HEAD_DIM) == (c >= HEAD_DIM), 1.0, 0.0).astype(BF16)


def _ada_kernel(c_ref, w_ref, b_ref, o_ref):
    c_act = _silu(c_ref[...])
    w_hi, w_lo = _split2(w_ref[0])
    o_ref[0] = _dot_x3(c_act, w_hi, w_lo) + b_ref[0]


def _ada_mod(c, ada_w, ada_b):
    depth, d, d3 = ada_w.shape
    b = c.shape[0]
    rows = 8
    c8 = jnp.zeros((rows, d), F32).at[:b].set(c)
    out = pl.pallas_call(
        _ada_kernel,
        grid=(depth, d3 // d),
        in_specs=[
            pl.BlockSpec((rows, d), lambda l, j: (0, 0)),
            pl.BlockSpec((1, d, d), lambda l, j: (l, 0, j)),
            pl.BlockSpec((1, 1, d), lambda l, j: (l, 0, j)),
        ],
        out_specs=pl.BlockSpec((1, rows, d), lambda l, j: (l, 0, j)),
        out_shape=jax.ShapeDtypeStruct((depth, rows, d3), F32),
        name="ada_mod",
    )(c8, ada_w, ada_b.reshape(depth, 1, d3))
    return out[:, :b]


def _in_proj_kernel(x_ref, g_ref, sc_ref, sh_ref, w_ref, rw_ref, zz_ref, sbq_ref, sbg_ref, *, widths):
    rw_w, zz_w, sbq_w, sbg_w = widths
    x = x_ref[0]
    ms = jnp.mean(x * x, axis=-1, keepdims=True)
    h = x * lax.rsqrt(ms + NORM_EPS) * g_ref[...]
    h = h * (1.0 + sc_ref[0]) + sh_ref[0]
    hb = h.astype(BF16)

    def emit(out_ref, out_col, w_col, width):
        step = 512
        for c0 in range(0, width, step):
            cw = min(step, width - c0)
            res = _dot(hb, w_ref[:, w_col + c0:w_col + c0 + cw])
            out_ref[0, :, out_col + c0:out_col + c0 + cw] = res.astype(out_ref.dtype)

    shift_rkv = rw_w - sbg_w
    emit(rw_ref, 0, 0, shift_rkv)
    emit(zz_ref, 0, shift_rkv, zz_w)
    emit(rw_ref, shift_rkv, shift_rkv + zz_w, sbg_w)
    sb0 = rw_w + zz_w
    emit(sbq_ref, 0, sb0, sbq_w)
    emit(sbg_ref, 0, sb0 + sbq_w, sbg_w)


def _in_proj(x, g, scale, shift, w_bf16, tm):
    b, s, d = x.shape
    width = d // 2
    widths = (4 * width, DECAY_LORA + ICLR_LORA, 3 * width, width)
    n_cols = w_bf16.shape[1]
    assert sum(widths) == n_cols
    return pl.pallas_call(
        functools.partial(_in_proj_kernel, widths=widths),
        grid=(b, s // tm),
        in_specs=[
            pl.BlockSpec((1, tm, d), lambda i, j: (i, j, 0)),
            pl.BlockSpec((1, d), lambda i, j: (0, 0)),
            pl.BlockSpec((1, 1, d), lambda i, j: (i, 0, 0)),
            pl.BlockSpec((1, 1, d), lambda i, j: (i, 0, 0)),
            pl.BlockSpec((d, n_cols), lambda i, j: (0, 0)),
        ],
        out_specs=[
            pl.BlockSpec((1, tm, widths[0]), lambda i, j: (i, j, 0)),
            pl.BlockSpec((1, tm, widths[1]), lambda i, j: (i, j, 0)),
            pl.BlockSpec((1, tm, widths[2]), lambda i, j: (i, j, 0)),
            pl.BlockSpec((1, tm, widths[3]), lambda i, j: (i, j, 0)),
        ],
        out_shape=[
            jax.ShapeDtypeStruct((b, s, widths[0]), F32),
            jax.ShapeDtypeStruct((b, s, widths[1]), F32),
            jax.ShapeDtypeStruct((b, s, widths[2]), BF16),
            jax.ShapeDtypeStruct((b, s, widths[3]), F32),
        ],
        compiler_params=pltpu.CompilerParams(
            dimension_semantics=("parallel", "parallel"), vmem_limit_bytes=VMEM_LIMIT_BYTES),
        name="in_proj",
    )(x, g.reshape(1, d), scale.reshape(b, 1, d), shift.reshape(b, 1, d), w_bf16)


def _rwkv_kernel(rkv_ref, zz_ref, g_ref, mu_ref, muz_ref, wl_ref, vec_ref, y_ref,
                 state_ref, prev_ref, prevz_ref, *, n_blocks):
    L = RWKV_CHUNK
    n_batch, rows = rkv_ref.shape[0], rkv_ref.shape[1]
    n_sub = rows // L
    width = n_blocks * LANES
    t = pl.program_id(0)

    @pl.when(t == 0)
    def _():
        state_ref[...] = jnp.zeros_like(state_ref)
        prev_ref[...] = jnp.zeros_like(prev_ref)
        prevz_ref[...] = jnp.zeros_like(prevz_ref)

    def token_shift(x, carry_ref, b, mu):
        row = lax.broadcasted_iota(jnp.int32, x.shape, 0)
        prev = jnp.where(row == 0, carry_ref[b, 7:8, :], pltpu.roll(x, 1, 0))
        carry_ref[b] = x[rows - 8:rows]
        return x + (prev - x) * mu

    k_k = vec_ref[0:1, :]
    k_a = vec_ref[1:2, :]
    r_k = vec_ref[2:3, :]
    ln_w = vec_ref[3:4, :]
    ln_b = vec_ref[4:5, :]
    w0 = vec_ref[5:6, :]
    a0 = vec_ref[6:7, :]

    lane = lax.broadcasted_iota(jnp.int32, (L, LANES), 1)
    head0 = lane < HEAD_DIM
    tr = lax.broadcasted_iota(jnp.int32, (L, 3 * L), 0)
    tc = lax.broadcasted_iota(jnp.int32, (L, 3 * L), 1)
    tc = jnp.where(tc >= 2 * L, tc - 2 * L, jnp.where(tc >= L, tc - L, tc))
    cum3 = jnp.where(tc <= tr, 1.0, 0.0).astype(BF16)
    pr = lax.broadcasted_iota(jnp.int32, (L, 2 * L), 0)
    pc = lax.broadcasted_iota(jnp.int32, (L, 2 * L), 1)
    pc = jnp.where(pc >= L, pc - L, pc)
    strict = pc < pr
    incl = pc <= pr
    eye2 = jnp.where(pc == pr, 1.0, 0.0)
    br = lax.broadcasted_iota(jnp.int32, (LANES, LANES), 0)
    bc = lax.broadcasted_iota(jnp.int32, (LANES, LANES), 1)
    same_head = (br >= HEAD_DIM) == (bc >= HEAD_DIM)

    def stacked(x):
        xb = x.astype(BF16)
        zero = jnp.zeros_like(xb)
        return jnp.concatenate([jnp.where(head0, xb, zero), jnp.where(head0, zero, xb)], axis=0)

    def head_sum(x):
        s0 = jnp.sum(jnp.where(head0, x, 0.0), axis=1, keepdims=True)
        s1 = jnp.sum(jnp.where(head0, 0.0, x), axis=1, keepdims=True)
        return jnp.where(head0, s0, s1)

    batches = range(n_batch)
    lane_z = lax.broadcasted_iota(jnp.int32, (rows, LANES), 1)
    xs, lora = [], []
    for b in batches:
        xs.append(token_shift(rkv_ref[b], prev_ref, b, mu_ref[...]))
        zs = token_shift(zz_ref[b], prevz_ref, b, muz_ref[...])
        act = jnp.where(lane_z < DECAY_LORA, jnp.tanh(zs), zs)
        a_hi, a_lo = _split2(act)
        lora.append(_dot(jnp.concatenate([a_hi, a_lo, a_hi], axis=1), wl_ref[...]))

    chains = [(b, p, q) for q in range(n_sub) for b in batches for p in range(n_blocks)]
    ids = range(len(chains))
    sls = [slice(p * LANES, (p + 1) * LANES) for _, p, _ in chains]
    rws = [slice(q * L, (q + 1) * L) for _, _, q in chains]
    r = [xs[b][rws[i], p * LANES:(p + 1) * LANES] for i, (b, p, _) in enumerate(chains)]
    k = [xs[b][rws[i], width + p * LANES:width + (p + 1) * LANES] for i, (b, p, _) in enumerate(chains)]
    v = [xs[b][rws[i], 2 * width + p * LANES:2 * width + (p + 1) * LANES] for i, (b, p, _) in enumerate(chains)]
    lw, a = [], []
    for i, (b, p, _) in enumerate(chains):
        lw.append(-DECAY_SCALE * _sigmoid(w0[:, sls[i]] + lora[b][rws[i], sls[i]]))
        a.append(_sigmoid(a0[:, sls[i]] + lora[b][rws[i], width + p * LANES:width + (p + 1) * LANES]))

    c = [_dot(cum3, jnp.concatenate(_split3(lw[i]), axis=0)) for i in ids]
    kk_raw = [k[i] * k_k[:, sls[i]] for i in ids]
    kk_ss = [head_sum(kk_raw[i] * kk_raw[i]) for i in ids]
    kmod = [k[i] * (1.0 + (a[i] - 1.0) * k_a[:, sls[i]]) for i in ids]
    bonus_dot = [head_sum(r[i] * kmod[i] * r_k[:, sls[i]]) for i in ids]

    lhs, rhs, xa_s, xr_b, v_s, upd, decay_last = [], [], [], [], [], [], []
    for i in ids:
        kk = kk_raw[i] * lax.rsqrt(jnp.maximum(kk_ss[i], 1e-24))
        bvec = kk * a[i]
        c_last = c[i][L - 1:L, :]
        g_in = jnp.exp(c[i])
        g_ex = jnp.exp(c[i] - lw[i])
        g_inv = jnp.exp(-c[i])
        g_rem = jnp.exp(c_last - c[i])
        xa = -kk * g_ex
        xr_b.append((r[i] * g_in).astype(BF16))
        xa_s.append(stacked(xa))
        v_s.append(stacked(v[i]))
        lhs.append(jnp.concatenate([xa.astype(BF16), xr_b[i]], axis=0))
        rhs.append(jnp.concatenate([stacked(bvec * g_inv), stacked(kmod[i] * g_inv)], axis=0))
        upd.append(jnp.concatenate([bvec * g_rem, kmod[i] * g_rem], axis=0).astype(BF16))
        decay_last.append(jnp.exp(c_last))

    gram = [_dot(lhs[i], rhs[i], NT) for i in ids]
    n_p = [jnp.where(strict, gram[i][:L, :2 * L], 0.0) for i in ids]
    a_ak = [jnp.where(strict, gram[i][:L, 2 * L:], 0.0).astype(BF16) for i in ids]
    a_rbk = [jnp.concatenate([jnp.where(incl, gram[i][L:, :2 * L], 0.0),
                              jnp.where(incl, gram[i][L:, 2 * L:], 0.0)], axis=1).astype(BF16) for i in ids]

    n_sq = L.bit_length() - 1
    qmat = [_dot(n_p[i].astype(BF16), stacked(n_p[i])) for i in ids]
    w1 = [_dot(a_ak[i], v_s[i]) for i in ids]
    pmat = [eye2 + n_p[i] for i in ids]
    for step in range(1, n_sq):
        q_s = [stacked(qmat[i]) for i in ids]
        if step < n_sq - 1:
            both = [_dot(jnp.concatenate([pmat[i], qmat[i]], axis=0).astype(BF16), q_s[i]) for i in ids]
            pmat = [pmat[i] + both[i][:L] for i in ids]
            qmat = [both[i][L:] for i in ids]
        else:
            pq = [_dot(pmat[i].astype(BF16), q_s[i]) for i in ids]
            pmat = [pmat[i] + pq[i] for i in ids]

    aw = [_dot(pmat[i].astype(BF16), jnp.concatenate([xa_s[i], stacked(w1[i])], axis=1)) for i in ids]
    n_state = n_batch * n_blocks
    h_t = [state_ref[j] for j in range(n_state)]
    y = []
    for q in range(n_sub):
        sub = range(q * n_state, (q + 1) * n_state)
        s9 = [_dot(jnp.concatenate([aw[i][:, :LANES].astype(BF16), xr_b[i]], axis=0),
                   h_t[i - q * n_state].astype(BF16), NT) for i in sub]
        u = [s9[j][:L] + aw[i][:, LANES:] for j, i in enumerate(sub)]
        y += [s9[j][L:] + _dot(a_rbk[i], jnp.concatenate([stacked(u[j]), v_s[i]], axis=0))
              for j, i in enumerate(sub)]
        for j, i in enumerate(sub):
            uv = jnp.concatenate([u[j], v[i]], axis=0).astype(BF16)
            h_t[j] = h_t[j] * decay_last[i] + jnp.where(same_head, _dot(uv, upd[i], TN), 0.0)
    for j in range(n_state):
        state_ref[j] = h_t[j]

    mean = [head_sum(y[i]) * (1.0 / HEAD_DIM) for i in ids]
    dlt = [y[i] - mean[i] for i in ids]
    var = [head_sum(dlt[i] * dlt[i]) * (1.0 / HEAD_DIM) for i in ids]
    for i, (b, p, _) in enumerate(chains):
        yn = dlt[i] * lax.rsqrt(var[i] + GN_EPS) * ln_w[:, sls[i]] + ln_b[:, sls[i]]
        gate = g_ref[b, rws[i], sls[i]]
        y_ref[b, rws[i], sls[i]] = ((yn + bonus_dot[i] * v[i]) * _silu(gate)).astype(y_ref.dtype)


def _rwkv(rw, zz, mu, w0, w2, a0, a2, k_k, k_a, r_k, ln_w, ln_b):
    b, s, w4 = rw.shape
    width = w4 // 4
    n_blocks = width // LANES
    L = RWKV_CHUNK
    mu_rkv = mu[:3 * width].reshape(1, 3 * width)
    mu_z = mu[3 * width:].reshape(1, DECAY_LORA + ICLR_LORA)
    w_lora = jnp.zeros((DECAY_LORA + ICLR_LORA, 2 * width), F32)
    w_lora = w_lora.at[:DECAY_LORA, :width].set(w2).at[DECAY_LORA:, width:].set(a2)
    wl_hi = w_lora.astype(BF16)
    wl_lo = (w_lora - wl_hi.astype(F32)).astype(BF16)
    wl_cat = jnp.concatenate([wl_hi, wl_hi, wl_lo], axis=0)
    vec = jnp.stack([k_k, k_a, r_k.reshape(width), ln_w, ln_b, w0, a0, jnp.zeros_like(w0)], axis=0)
    rows = L * RWKV_STEP_CHUNKS
    assert s % rows == 0
    return pl.pallas_call(
        functools.partial(_rwkv_kernel, n_blocks=n_blocks),
        grid=(s // rows,),
        in_specs=[
            pl.BlockSpec((b, rows, 3 * width), lambda j: (0, j, 0)),
            pl.BlockSpec((b, rows, LANES), lambda j: (0, j, 0)),
            pl.BlockSpec((b, rows, width), lambda j: (0, j, 3)),
            pl.BlockSpec((1, 3 * width), lambda j: (0, 0)),
            pl.BlockSpec((1, LANES), lambda j: (0, 0)),
            pl.BlockSpec((3 * LANES, 2 * width), lambda j: (0, 0)),
            pl.BlockSpec((8, width), lambda j: (0, 0)),
        ],
        out_specs=pl.BlockSpec((b, rows, width), lambda j: (0, j, 0)),
        out_shape=jax.ShapeDtypeStruct((b, s, width), BF16),
        scratch_shapes=[
            pltpu.VMEM((b * n_blocks, LANES, LANES), F32),
            pltpu.VMEM((b, 8, 3 * width), F32),
            pltpu.VMEM((b, 8, LANES), F32),
        ],
        compiler_params=pltpu.CompilerParams(
            dimension_semantics=("arbitrary",), vmem_limit_bytes=VMEM_LIMIT_BYTES),
        name="rwkv7",
    )(rw, zz, rw, mu_rkv, mu_z, wl_cat, vec)


def _sb_kernel(q_ref, k_ref, v_ref, sg_ref, o_ref, acc_ref, carry_ref, *, n_blocks):
    T = SB_BLOCK
    n_sub = q_ref.shape[1] // T
    step = pl.program_id(0)
    chains = [(h, b, slice(p * LANES, (p + 1) * LANES))
              for h in range(n_sub) for b in range(q_ref.shape[0]) for p in range(n_blocks)]
    ids = range(len(chains))
    lane = lax.broadcasted_iota(jnp.int32, (T, LANES), 1)
    head0 = lane < HEAD_DIM
    scale = HEAD_DIM ** -0.5
    qs = []
    for h, b, cols in chains:
        q = q_ref[b, h * T:(h + 1) * T, cols]
        zero = jnp.zeros_like(q)
        qs.append(jnp.concatenate([jnp.where(head0, q, zero), jnp.where(head0, zero, q)], axis=0) * scale)

    sr = lax.broadcasted_iota(jnp.int32, (T, T), 0)
    sc = lax.broadcasted_iota(jnp.int32, (T, T), 1)
    tri_ones = jnp.concatenate([jnp.where(sr > sc, 1.0, 0.0), jnp.ones((T, T), F32)], axis=1).astype(BF16)
    tri_ones2 = jnp.concatenate([tri_ones, tri_ones], axis=0)
    mr = lax.broadcasted_iota(jnp.int32, (2 * T, T), 0)
    mc = lax.broadcasted_iota(jnp.int32, (2 * T, T), 1)
    causal = mc < jnp.where(mr >= T, mr - T, mr)

    def tiles(sel, back, diag):
        starts = [pl.multiple_of((step * n_sub + h - back) * T, T) for h in range(n_sub)]
        z = {i: _dot(qs[i], k_ref[chains[i][1], pl.ds(starts[chains[i][0]], T), chains[i][2]], NT)
             for i in sel}
        neg_1m = {i: jnp.maximum(z[i], 0.0) + jnp.log(1.0 + jnp.exp(-jnp.abs(z[i]))) for i in sel}
        masked = {i: jnp.where(causal, neg_1m[i], 0.0) for i in sel} if diag else neg_1m
        sums = {i: _dot(jnp.concatenate(_split2(masked[i]), axis=1), tri_ones2) for i in sel}
        for i in sel:
            h, b, cols = chains[i]
            log_b = z[i] - neg_1m[i]
            if diag:
                attn = jnp.where(causal, jnp.exp(log_b - sums[i][:, :T]), 0.0)
                acc_ref[i] = _dot(attn.astype(BF16), v_ref[b, pl.ds(starts[h], T), cols])
                carry_ref[i] = sums[i][:, T:]
            else:
                carry = carry_ref[i]
                attn = jnp.exp(log_b - sums[i][:, :T] - carry)
                acc_ref[i] += _dot(attn.astype(BF16), v_ref[b, pl.ds(starts[h], T), cols])
                carry_ref[i] = carry + sums[i][:, T:]

    def live():
        low = carry_ref[0]
        for i in ids[1:]:
            low = jnp.minimum(low, carry_ref[i])
        return (jnp.min(low) < SB_F32_EXP_UNDERFLOW).astype(jnp.int32)

    tiles(ids, 0, True)
    n_common = step * n_sub

    def cond(state):
        it, alive = state
        return jnp.logical_and(it < n_common, alive > 0)

    def body(state):
        it, _ = state
        tiles(ids, it + 1, False)
        return it + 1, live()

    _, alive = lax.while_loop(cond, body, (jnp.int32(0), live()))

    for extra in range(1, n_sub):
        @pl.when(alive > 0)
        def _():
            tiles([i for i in ids if chains[i][0] >= extra], n_common + extra, False)

    head_ones = _head_ones()
    for i, (h, b, cols) in enumerate(chains):
        acc = acc_ref[i]
        o = jnp.where(head0, acc[:T], acc[T:])
        ms = _dot_exact_rhs(o * o, head_ones) * (1.0 / HEAD_DIM)
        o_ref[b, h * T:(h + 1) * T, cols] = o * lax.rsqrt(ms + NORM_EPS) * sg_ref[:, cols]


def _stick_breaking(qkv, sb_g):
    b, s, w3 = qkv.shape
    width = w3 // 3
    n_blocks = width // LANES
    T = SB_BLOCK
    resident = pl.Buffered(1)
    rows = T * SB_STEP_BLOCKS
    assert s % rows == 0
    n_chains = SB_STEP_BLOCKS * b * n_blocks
    return pl.pallas_call(
        functools.partial(_sb_kernel, n_blocks=n_blocks),
        grid=(s // rows,),
        in_specs=[
            pl.BlockSpec((b, rows, width), lambda j: (0, j, 0)),
            pl.BlockSpec((b, s, width), lambda j: (0, 0, 1), pipeline_mode=resident),
            pl.BlockSpec((b, s, width), lambda j: (0, 0, 2), pipeline_mode=resident),
            pl.BlockSpec((1, width), lambda j: (0, 0)),
        ],
        out_specs=pl.BlockSpec((b, rows, width), lambda j: (0, j, 0)),
        out_shape=jax.ShapeDtypeStruct((b, s, width), F32),
        scratch_shapes=[
            pltpu.VMEM((n_chains, 2 * T, LANES), F32),
            pltpu.VMEM((n_chains, 2 * T, T), F32),
        ],
        compiler_params=pltpu.CompilerParams(
            dimension_semantics=("arbitrary",), vmem_limit_bytes=VMEM_LIMIT_BYTES),
        name="stick_breaking",
    )(qkv, qkv, qkv, sb_g.reshape(1, width))


def _out_proj_kernel(yr_ref, os_ref, sg_ref, w_ref, x_ref, gate_ref, fg_ref, o_ref, perm_ref, *, final_norm):
    half = yr_ref.shape[-1]
    n_qb, sub = os_ref.shape[1], os_ref.shape[2]
    o_tok = os_ref[0].reshape(n_qb * sub, half)
    cols = []
    for cb in range(half // LANES):
        perm_ref[cb] = o_tok[:, cb * LANES:(cb + 1) * LANES]
        cols.append(jnp.concatenate(
            [perm_ref[cb, pl.ds(ii, n_qb, stride=sub), :] for ii in range(sub)], axis=0))
    o_rows = jnp.concatenate(cols, axis=1)
    y_sb = (o_rows * _silu(sg_ref[0])).astype(BF16)
    y = _dot(yr_ref[0], w_ref[:half, :]) + _dot(y_sb, w_ref[half:, :])
    x = x_ref[0] + gate_ref[0] * y
    if final_norm:
        ms = jnp.mean(x * x, axis=-1, keepdims=True)
        x = x * lax.rsqrt(ms + NORM_EPS) * fg_ref[...]
    o_ref[0] = x


def _out_proj(y_rw, o_sb, sb_gate, w_bf16, x, gate, final_g, final_norm):
    b, s, d = x.shape
    half = y_rw.shape[-1]
    n_qb = s // SB_BLOCK
    sub = 8
    tm = sub * n_qb
    o_sb4 = o_sb.reshape(b, n_qb, SB_BLOCK, half)
    return pl.pallas_call(
        functools.partial(_out_proj_kernel, final_norm=final_norm),
        grid=(b, s // tm),
        in_specs=[
            pl.BlockSpec((1, tm, half), lambda i, j: (i, j, 0)),
            pl.BlockSpec((1, n_qb, sub, half), lambda i, j: (i, 0, j, 0)),
            pl.BlockSpec((1, tm, half), lambda i, j: (i, j, 0)),
            pl.BlockSpec((2 * half, d), lambda i, j: (0, 0)),
            pl.BlockSpec((1, tm, d), lambda i, j: (i, j, 0)),
            pl.BlockSpec((1, 1, d), lambda i, j: (i, 0, 0)),
            pl.BlockSpec((1, d), lambda i, j: (0, 0)),
        ],
        out_specs=pl.BlockSpec((1, tm, d), lambda i, j: (i, j, 0)),
        out_shape=jax.ShapeDtypeStruct((b, s, d), F32),
        scratch_shapes=[pltpu.VMEM((half // LANES, tm, LANES), F32)],
        compiler_params=pltpu.CompilerParams(
            dimension_semantics=("parallel", "parallel"), vmem_limit_bytes=VMEM_LIMIT_BYTES),
        name="out_proj",
    )(y_rw, o_sb4, sb_gate, w_bf16, x, gate.reshape(b, 1, d), final_g.reshape(1, d))


def _row_tile(s):
    for tm in (512, 256, 128, 64, 32, 16, 8):
        if s % tm == 0:
            return tm
    raise ValueError(f"sequence length {s} must be a multiple of 8")


def kernel(x, c, norm_g, ada_w, ada_b, w_in, w_out, tshift_mu, decay_w0, decay_w2, iclr_a0, iclr_a2,
           k_k, k_a, r_k, rwkv_ln_w, rwkv_ln_b, sb_norm_g, final_g):
    depth = norm_g.shape[0]
    b, s, d = x.shape
    assert s % SB_BLOCK == 0 and s % RWKV_CHUNK == 0
    tm = _row_tile(s)
    mod = _ada_mod(c, ada_w, ada_b)
    w_in_b = w_in.astype(BF16)
    w_out_b = w_out.astype(BF16)
    for l in range(depth):
        shift, scale, gate = mod[l, :, :d], mod[l, :, d:2 * d], mod[l, :, 2 * d:]
        rw, zz, sb_qkv, sb_gate = _in_proj(x, norm_g[l], scale, shift, w_in_b[l], tm)
        y_rw = _rwkv(rw, zz, tshift_mu[l], decay_w0[l], decay_w2[l], iclr_a0[l], iclr_a2[l],
                     k_k[l], k_a[l], r_k[l], rwkv_ln_w[l], rwkv_ln_b[l])
        o_sb = _stick_breaking(sb_qkv, sb_norm_g[l])
        x = _out_proj(y_rw, o_sb, sb_gate, w_out_b[l], x, gate, final_g, l == depth - 1)
    return x
```

```python
import functools

import jax
import jax.numpy as jnp
from jax import lax
from jax.experimental import pallas as pl
from jax.experimental.pallas import tpu as pltpu

F32 = jnp.float32
BF16 = jnp.bfloat16

HEAD_DIM = 64
LANES = 128
HEADS_PER_BLOCK = LANES // HEAD_DIM
RWKV_CHUNK = 64
RWKV_STEP_CHUNKS = 8
SB_BLOCK = 128
SB_STEP_BLOCKS = 2
SB_F32_EXP_UNDERFLOW = 106.0
DECAY_SCALE = 0.6065306597126334
DECAY_LORA = 64
ICLR_LORA = 64
NORM_EPS = 1e-6
GN_EPS = 64e-5
VMEM_LIMIT_BYTES = 56 * 1024 * 1024

NN = (((1,), (0,)), ((), ()))
NT = (((1,), (1,)), ((), ()))
TN = (((0,), (0,)), ((), ()))


def _dot(a, b, dims=NN):
    return lax.dot_general(a, b, dims, preferred_element_type=F32)


def _split2(x):
    hi = x.astype(BF16)
    lo = (x - hi.astype(F32)).astype(BF16)
    return hi, lo


def _split3(x):
    hi = x.astype(BF16)
    r1 = x - hi.astype(F32)
    mid = r1.astype(BF16)
    lo = (r1 - mid.astype(F32)).astype(BF16)
    return hi, mid, lo


def _dot_exact_rhs(x, rhs_bf16, dims=NN):
    hi, mid, lo = _split3(x)
    return _dot(hi, rhs_bf16, dims) + _dot(mid, rhs_bf16, dims) + _dot(lo, rhs_bf16, dims)


def _dot_x3(a, b_hi, b_lo):
    a_hi, a_lo = _split2(a)
    return _dot(a_hi, b_hi) + _dot(a_lo, b_hi) + _dot(a_hi, b_lo)


def _sigmoid(u):
    return 1.0 / (1.0 + jnp.exp(-u))


def _silu(u):
    return u * _sigmoid(u)


def _head_ones():
    r = lax.broadcasted_iota(jnp.int32, (LANES, LANES), 0)
    c = lax.broadcasted_iota(jnp.int32, (LANES, LANES), 1)
    return jnp.where((r >= HEAD_DIM) == (c >= HEAD_DIM), 1.0, 0.0).astype(BF16)


def _ada_kernel(c_ref, w_ref, b_ref, o_ref):
    c_act = _silu(c_ref[...])
    w_hi, w_lo = _split2(w_ref[0])
    o_ref[0] = _dot_x3(c_act, w_hi, w_lo) + b_ref[0]


def _ada_mod(c, ada_w, ada_b):
    depth, d, d3 = ada_w.shape
    b = c.shape[0]
    rows = 8
    c8 = jnp.zeros((rows, d), F32).at[:b].set(c)
    out = pl.pallas_call(
        _ada_kernel,
        grid=(depth, d3 // d),
        in_specs=[
            pl.BlockSpec((rows, d), lambda l, j: (0, 0)),
            pl.BlockSpec((1, d, d), lambda l, j: (l, 0, j)),
            pl.BlockSpec((1, 1, d), lambda l, j: (l, 0, j)),
        ],
        out_specs=pl.BlockSpec((1, rows, d), lambda l, j: (l, 0, j)),
        out_shape=jax.ShapeDtypeStruct((depth, rows, d3), F32),
        name="ada_mod",
    )(c8, ada_w, ada_b.reshape(depth, 1, d3))
    return out[:, :b]


def _in_proj_kernel(x_ref, g_ref, sc_ref, sh_ref, w_ref, rw_ref, zz_ref, sbq_ref, sbg_ref, *, widths):
    rw_w, zz_w, sbq_w, sbg_w = widths
    x = x_ref[0]
    ms = jnp.mean(x * x, axis=-1, keepdims=True)
    h = x * lax.rsqrt(ms + NORM_EPS) * g_ref[...]
    h = h * (1.0 + sc_ref[0]) + sh_ref[0]
    hb = h.astype(BF16)

    def emit(out_ref, out_col, w_col, width):
        step = 512
        for c0 in range(0, width, step):
            cw = min(step, width - c0)
            res = _dot(hb, w_ref[:, w_col + c0:w_col + c0 + cw])
            out_ref[0, :, out_col + c0:out_col + c0 + cw] = res.astype(out_ref.dtype)

    shift_rkv = rw_w - sbg_w
    emit(rw_ref, 0, 0, shift_rkv)
    emit(zz_ref, 0, shift_rkv, zz_w)
    emit(rw_ref, shift_rkv, shift_rkv + zz_w, sbg_w)
    sb0 = rw_w + zz_w
    emit(sbq_ref, 0, sb0, sbq_w)
    emit(sbg_ref, 0, sb0 + sbq_w, sbg_w)


def _in_proj(x, g, scale, shift, w_bf16, tm):
    b, s, d = x.shape
    width = d // 2
    widths = (4 * width, DECAY_LORA + ICLR_LORA, 3 * width, width)
    n_cols = w_bf16.shape[1]
    assert sum(widths) == n_cols
    return pl.pallas_call(
        functools.partial(_in_proj_kernel, widths=widths),
        grid=(b, s // tm),
        in_specs=[
            pl.BlockSpec((1, tm, d), lambda i, j: (i, j, 0)),
            pl.BlockSpec((1, d), lambda i, j: (0, 0)),
            pl.BlockSpec((1, 1, d), lambda i, j: (i, 0, 0)),
            pl.BlockSpec((1, 1, d), lambda i, j: (i, 0, 0)),
            pl.BlockSpec((d, n_cols), lambda i, j: (0, 0)),
        ],
        out_specs=[
            pl.BlockSpec((1, tm, widths[0]), lambda i, j: (i, j, 0)),
            pl.BlockSpec((1, tm, widths[1]), lambda i, j: (i, j, 0)),
            pl.BlockSpec((1, tm, widths[2]), lambda i, j: (i, j, 0)),
            pl.BlockSpec((1, tm, widths[3]), lambda i, j: (i, j, 0)),
        ],
        out_shape=[
            jax.ShapeDtypeStruct((b, s, widths[0]), F32),
            jax.ShapeDtypeStruct((b, s, widths[1]), F32),
            jax.ShapeDtypeStruct((b, s, widths[2]), BF16),
            jax.ShapeDtypeStruct((b, s, widths[3]), F32),
        ],
        compiler_params=pltpu.CompilerParams(
            dimension_semantics=("parallel", "parallel"), vmem_limit_bytes=VMEM_LIMIT_BYTES),
        name="in_proj",
    )(x, g.reshape(1, d), scale.reshape(b, 1, d), shift.reshape(b, 1, d), w_bf16)


def _rwkv_kernel(rkv_ref, zz_ref, g_ref, mu_ref, muz_ref, wl_ref, vec_ref, y_ref,
                 state_ref, prev_ref, prevz_ref, *, n_blocks):
    L = RWKV_CHUNK
    n_batch, rows = rkv_ref.shape[0], rkv_ref.shape[1]
    n_sub = rows // L
    width = n_blocks * LANES
    t = pl.program_id(0)

    @pl.when(t == 0)
    def _():
        state_ref[...] = jnp.zeros_like(state_ref)
        prev_ref[...] = jnp.zeros_like(prev_ref)
        prevz_ref[...] = jnp.zeros_like(prevz_ref)

    def token_shift(x, carry_ref, b, mu):
        row = lax.broadcasted_iota(jnp.int32, x.shape, 0)
        prev = jnp.where(row == 0, carry_ref[b, 7:8, :], pltpu.roll(x, 1, 0))
        carry_ref[b] = x[rows - 8:rows]
        return x + (prev - x) * mu

    k_k = vec_ref[0:1, :]
    k_a = vec_ref[1:2, :]
    r_k = vec_ref[2:3, :]
    ln_w = vec_ref[3:4, :]
    ln_b = vec_ref[4:5, :]
    w0 = vec_ref[5:6, :]
    a0 = vec_ref[6:7, :]

    lane = lax.broadcasted_iota(jnp.int32, (L, LANES), 1)
    head0 = lane < HEAD_DIM
    tr = lax.broadcasted_iota(jnp.int32, (L, 3 * L), 0)
    tc = lax.broadcasted_iota(jnp.int32, (L, 3 * L), 1)
    tc = jnp.where(tc >= 2 * L, tc - 2 * L, jnp.where(tc >= L, tc - L, tc))
    cum3 = jnp.where(tc <= tr, 1.0, 0.0).astype(BF16)
    pr = lax.broadcasted_iota(jnp.int32, (L, 2 * L), 0)
    pc = lax.broadcasted_iota(jnp.int32, (L, 2 * L), 1)
    pc = jnp.where(pc >= L, pc - L, pc)
    strict = pc < pr
    incl = pc <= pr
    eye2 = jnp.where(pc == pr, 1.0, 0.0)
    br = lax.broadcasted_iota(jnp.int32, (LANES, LANES), 0)
    bc = lax.broadcasted_iota(jnp.int32, (LANES, LANES), 1)
    same_head = (br >= HEAD_DIM) == (bc >= HEAD_DIM)

    def stacked(x):
        xb = x.astype(BF16)
        zero = jnp.zeros_like(xb)
        return jnp.concatenate([jnp.where(head0, xb, zero), jnp.where(head0, zero, xb)], axis=0)

    def head_sum(x):
        s0 = jnp.sum(jnp.where(head0, x, 0.0), axis=1, keepdims=True)
        s1 = jnp.sum(jnp.where(head0, 0.0, x), axis=1, keepdims=True)
        return jnp.where(head0, s0, s1)

    batches = range(n_batch)
    lane_z = lax.broadcasted_iota(jnp.int32, (rows, LANES), 1)
    xs, lora = [], []
    for b in batches:
        xs.append(token_shift(rkv_ref[b], prev_ref, b, mu_ref[...]))
        zs = token_shift(zz_ref[b], prevz_ref, b, muz_ref[...])
        act = jnp.where(lane_z < DECAY_LORA, jnp.tanh(zs), zs)
        a_hi, a_lo = _split2(act)
        lora.append(_dot(jnp.concatenate([a_hi, a_lo, a_hi], axis=1), wl_ref[...]))

    chains = [(b, p, q) for q in range(n_sub) for b in batches for p in range(n_blocks)]
    ids = range(len(chains))
    sls = [slice(p * LANES, (p + 1) * LANES) for _, p, _ in chains]
    rws = [slice(q * L, (q + 1) * L) for _, _, q in chains]
    r = [xs[b][rws[i], p * LANES:(p + 1) * LANES] for i, (b, p, _) in enumerate(chains)]
    k = [xs[b][rws[i], width + p * LANES:width + (p + 1) * LANES] for i, (b, p, _) in enumerate(chains)]
    v = [xs[b][rws[i], 2 * width + p * LANES:2 * width + (p + 1) * LANES] for i, (b, p, _) in enumerate(chains)]
    lw, a = [], []
    for i, (b, p, _) in enumerate(chains):
        lw.append(-DECAY_SCALE * _sigmoid(w0[:, sls[i]] + lora[b][rws[i], sls[i]]))
        a.append(_sigmoid(a0[:, sls[i]] + lora[b][rws[i], width + p * LANES:width + (p + 1) * LANES]))

    c = [_dot(cum3, jnp.concatenate(_split3(lw[i]), axis=0)) for i in ids]
    kk_raw = [k[i] * k_k[:, sls[i]] for i in ids]
    kk_ss = [head_sum(kk_raw[i] * kk_raw[i]) for i in ids]
    kmod = [k[i] * (1.0 + (a[i] - 1.0) * k_a[:, sls[i]]) for i in ids]
    bonus_dot = [head_sum(r[i] * kmod[i] * r_k[:, sls[i]]) for i in ids]

    lhs, rhs, xa_s, xr_b, v_s, upd, decay_last = [], [], [], [], [], [], []
    for i in ids:
        kk = kk_raw[i] * lax.rsqrt(jnp.maximum(kk_ss[i], 1e-24))
        bvec = kk * a[i]
        c_last = c[i][L - 1:L, :]
        g_in = jnp.exp(c[i])
        g_ex = jnp.exp(c[i] - lw[i])
        g_inv = jnp.exp(-c[i])
        g_rem = jnp.exp(c_last - c[i])
        xa = -kk * g_ex
        xr_b.append((r[i] * g_in).astype(BF16))
        xa_s.append(stacked(xa))
        v_s.append(stacked(v[i]))
        lhs.append(jnp.concatenate([xa.astype(BF16), xr_b[i]], axis=0))
        rhs.append(jnp.concatenate([stacked(bvec * g_inv), stacked(kmod[i] * g_inv)], axis=0))
        upd.append(jnp.concatenate([bvec * g_rem, kmod[i] * g_rem], axis=0).astype(BF16))
        decay_last.append(jnp.exp(c_last))

    gram = [_dot(lhs[i], rhs[i], NT) for i in ids]
    n_p = [jnp.where(strict, gram[i][:L, :2 * L], 0.0) for i in ids]
    a_ak = [jnp.where(strict, gram[i][:L, 2 * L:], 0.0).astype(BF16) for i in ids]
    a_rbk = [jnp.concatenate([jnp.where(incl, gram[i][L:, :2 * L], 0.0),
                              jnp.where(incl, gram[i][L:, 2 * L:], 0.0)], axis=1).astype(BF16) for i in ids]

    n_sq = L.bit_length() - 1
    qmat = [_dot(n_p[i].astype(BF16), stacked(n_p[i])) for i in ids]
    w1 = [_dot(a_ak[i], v_s[i]) for i in ids]
    pmat = [eye2 + n_p[i] for i in ids]
    for step in range(1, n_sq):
        q_s = [stacked(qmat[i]) for i in ids]
        if step < n_sq - 1:
            both = [_dot(jnp.concatenate([pmat[i], qmat[i]], axis=0).astype(BF16), q_s[i]) for i in ids]
            pmat = [pmat[i] + both[i][:L] for i in ids]
            qmat = [both[i][L:] for i in ids]
        else:
            pq = [_dot(pmat[i].astype(BF16), q_s[i]) for i in ids]
            pmat = [pmat[i] + pq[i] for i in ids]

    aw = [_dot(pmat[i].astype(BF16), jnp.concatenate([xa_s[i], stacked(w1[i])], axis=1)) for i in ids]
    n_state = n_batch * n_blocks
    h_t = [state_ref[j] for j in range(n_state)]
    y = []
    for q in range(n_sub):
        sub = range(q * n_state, (q + 1) * n_state)
        s9 = [_dot(jnp.concatenate([aw[i][:, :LANES].astype(BF16), xr_b[i]], axis=0),
                   h_t[i - q * n_state].astype(BF16), NT) for i in sub]
        u = [s9[j][:L] + aw[i][:, LANES:] for j, i in enumerate(sub)]
        y += [s9[j][L:] + _dot(a_rbk[i], jnp.concatenate([stacked(u[j]), v_s[i]], axis=0))
              for j, i in enumerate(sub)]
        for j, i in enumerate(sub):
            uv = jnp.concatenate([u[j], v[i]], axis=0).astype(BF16)
            h_t[j] = h_t[j] * decay_last[i] + jnp.where(same_head, _dot(uv, upd[i], TN), 0.0)
    for j in range(n_state):
        state_ref[j] = h_t[j]

    mean = [head_sum(y[i]) * (1.0 / HEAD_DIM) for i in ids]
    dlt = [y[i] - mean[i] for i in ids]
    var = [head_sum(dlt[i] * dlt[i]) * (1.0 / HEAD_DIM) for i in ids]
    for i, (b, p, _) in enumerate(chains):
        yn = dlt[i] * lax.rsqrt(var[i] + GN_EPS) * ln_w[:, sls[i]] + ln_b[:, sls[i]]
        gate = g_ref[b, rws[i], sls[i]]
        y_ref[b, rws[i], sls[i]] = ((yn + bonus_dot[i] * v[i]) * _silu(gate)).astype(y_ref.dtype)


def _rwkv(rw, zz, mu, w0, w2, a0, a2, k_k, k_a, r_k, ln_w, ln_b):
    b, s, w4 = rw.shape
    width = w4 // 4
    n_blocks = width // LANES
    L = RWKV_CHUNK
    mu_rkv = mu[:3 * width].reshape(1, 3 * width)
    mu_z = mu[3 * width:].reshape(1, DECAY_LORA + ICLR_LORA)
    w_lora = jnp.zeros((DECAY_LORA + ICLR_LORA, 2 * width), F32)
    w_lora = w_lora.at[:DECAY_LORA, :width].set(w2).at[DECAY_LORA:, width:].set(a2)
    wl_hi = w_lora.astype(BF16)
    wl_lo = (w_lora - wl_hi.astype(F32)).astype(BF16)
    wl_cat = jnp.concatenate([wl_hi, wl_hi, wl_lo], axis=0)
    vec = jnp.stack([k_k, k_a, r_k.reshape(width), ln_w, ln_b, w0, a0, jnp.zeros_like(w0)], axis=0)
    rows = L * RWKV_STEP_CHUNKS
    assert s % rows == 0
    return pl.pallas_call(
        functools.partial(_rwkv_kernel, n_blocks=n_blocks),
        grid=(s // rows,),
        in_specs=[
            pl.BlockSpec((b, rows, 3 * width), lambda j: (0, j, 0)),
            pl.BlockSpec((b, rows, LANES), lambda j: (0, j, 0)),
            pl.BlockSpec((b, rows, width), lambda j: (0, j, 3)),
            pl.BlockSpec((1, 3 * width), lambda j: (0, 0)),
            pl.BlockSpec((1, LANES), lambda j: (0, 0)),
            pl.BlockSpec((3 * LANES, 2 * width), lambda j: (0, 0)),
            pl.BlockSpec((8, width), lambda j: (0, 0)),
        ],
        out_specs=pl.BlockSpec((b, rows, width), lambda j: (0, j, 0)),
        out_shape=jax.ShapeDtypeStruct((b, s, width), BF16),
        scratch_shapes=[
            pltpu.VMEM((b * n_blocks, LANES, LANES), F32),
            pltpu.VMEM((b, 8, 3 * width), F32),
            pltpu.VMEM((b, 8, LANES), F32),
        ],
        compiler_params=pltpu.CompilerParams(
            dimension_semantics=("arbitrary",), vmem_limit_bytes=VMEM_LIMIT_BYTES),
        name="rwkv7",
    )(rw, zz, rw, mu_rkv, mu_z, wl_cat, vec)


def _sb_kernel(q_ref, k_ref, v_ref, sg_ref, o_ref, acc_ref, carry_ref, *, n_blocks):
    T = SB_BLOCK
    n_sub = q_ref.shape[1] // T
    step = pl.program_id(0)
    chains = [(h, b, slice(p * LANES, (p + 1) * LANES))
              for h in range(n_sub) for b in range(q_ref.shape[0]) for p in range(n_blocks)]
    ids = range(len(chains))
    lane = lax.broadcasted_iota(jnp.int32, (T, LANES), 1)
    head0 = lane < HEAD_DIM
    scale = HEAD_DIM ** -0.5
    qs = []
    for h, b, cols in chains:
        q = q_ref[b, h * T:(h + 1) * T, cols]
        zero = jnp.zeros_like(q)
        qs.append(jnp.concatenate([jnp.where(head0, q, zero), jnp.where(head0, zero, q)], axis=0) * scale)

    sr = lax.broadcasted_iota(jnp.int32, (T, T), 0)
    sc = lax.broadcasted_iota(jnp.int32, (T, T), 1)
    tri_ones = jnp.concatenate([jnp.where(sr > sc, 1.0, 0.0), jnp.ones((T, T), F32)], axis=1).astype(BF16)
    tri_ones2 = jnp.concatenate([tri_ones, tri_ones], axis=0)
    mr = lax.broadcasted_iota(jnp.int32, (2 * T, T), 0)
    mc = lax.broadcasted_iota(jnp.int32, (2 * T, T), 1)
    causal = mc < jnp.where(mr >= T, mr - T, mr)

    def tiles(sel, back, diag):
        starts = [pl.multiple_of((step * n_sub + h - back) * T, T) for h in range(n_sub)]
        z = {i: _dot(qs[i], k_ref[chains[i][1], pl.ds(starts[chains[i][0]], T), chains[i][2]], NT)
             for i in sel}
        neg_1m = {i: jnp.maximum(z[i], 0.0) + jnp.log(1.0 + jnp.exp(-jnp.abs(z[i]))) for i in sel}
        masked = {i: jnp.where(causal, neg_1m[i], 0.0) for i in sel} if diag else neg_1m
        sums = {i: _dot(jnp.concatenate(_split2(masked[i]), axis=1), tri_ones2) for i in sel}
        for i in sel:
            h, b, cols = chains[i]
            log_b = z[i] - neg_1m[i]
            if diag:
                attn = jnp.where(causal, jnp.exp(log_b - sums[i][:, :T]), 0.0)
                acc_ref[i] = _dot(attn.astype(BF16), v_ref[b, pl.ds(starts[h], T), cols])
                carry_ref[i] = sums[i][:, T:]
            else:
                carry = carry_ref[i]
                attn = jnp.exp(log_b - sums[i][:, :T] - carry)
                acc_ref[i] += _dot(attn.astype(BF16), v_ref[b, pl.ds(starts[h], T), cols])
                carry_ref[i] = carry + sums[i][:, T:]

    def live():
        low = carry_ref[0]
        for i in ids[1:]:
            low = jnp.minimum(low, carry_ref[i])
        return (jnp.min(low) < SB_F32_EXP_UNDERFLOW).astype(jnp.int32)

    tiles(ids, 0, True)
    n_common = step * n_sub

    def cond(state):
        it, alive = state
        return jnp.logical_and(it < n_common, alive > 0)

    def body(state):
        it, _ = state
        tiles(ids, it + 1, False)
        return it + 1, live()

    _, alive = lax.while_loop(cond, body, (jnp.int32(0), live()))

    for extra in range(1, n_sub):
        @pl.when(alive > 0)
        def _():
            tiles([i for i in ids if chains[i][0] >= extra], n_common + extra, False)

    head_ones = _head_ones()
    for i, (h, b, cols) in enumerate(chains):
        acc = acc_ref[i]
        o = jnp.where(head0, acc[:T], acc[T:])
        ms = _dot_exact_rhs(o * o, head_ones) * (1.0 / HEAD_DIM)
        o_ref[b, h * T:(h + 1) * T, cols] = o * lax.rsqrt(ms + NORM_EPS) * sg_ref[:, cols]


def _stick_breaking(qkv, sb_g):
    b, s, w3 = qkv.shape
    width = w3 // 3
    n_blocks = width // LANES
    T = SB_BLOCK
    resident = pl.Buffered(1)
    rows = T * SB_STEP_BLOCKS
    assert s % rows == 0
    n_chains = SB_STEP_BLOCKS * b * n_blocks
    return pl.pallas_call(
        functools.partial(_sb_kernel, n_blocks=n_blocks),
        grid=(s // rows,),
        in_specs=[
            pl.BlockSpec((b, rows, width), lambda j: (0, j, 0)),
            pl.BlockSpec((b, s, width), lambda j: (0, 0, 1), pipeline_mode=resident),
            pl.BlockSpec((b, s, width), lambda j: (0, 0, 2), pipeline_mode=resident),
            pl.BlockSpec((1, width), lambda j: (0, 0)),
        ],
        out_specs=pl.BlockSpec((b, rows, width), lambda j: (0, j, 0)),
        out_shape=jax.ShapeDtypeStruct((b, s, width), F32),
        scratch_shapes=[
            pltpu.VMEM((n_chains, 2 * T, LANES), F32),
            pltpu.VMEM((n_chains, 2 * T, T), F32),
        ],
        compiler_params=pltpu.CompilerParams(
            dimension_semantics=("arbitrary",), vmem_limit_bytes=VMEM_LIMIT_BYTES),
        name="stick_breaking",
    )(qkv, qkv, qkv, sb_g.reshape(1, width))


def _out_proj_kernel(yr_ref, os_ref, sg_ref, w_ref, x_ref, gate_ref, fg_ref, o_ref, perm_ref, *, final_norm):
    half = yr_ref.shape[-1]
    n_qb, sub = os_ref.shape[1], os_ref.shape[2]
    o_tok = os_ref[0].reshape(n_qb * sub, half)
    cols = []
    for cb in range(half // LANES):
        perm_ref[cb] = o_tok[:, cb * LANES:(cb + 1) * LANES]
        cols.append(jnp.concatenate(
            [perm_ref[cb, pl.ds(ii, n_qb, stride=sub), :] for ii in range(sub)], axis=0))
    o_rows = jnp.concatenate(cols, axis=1)
    y_sb = (o_rows * _silu(sg_ref[0])).astype(BF16)
    y = _dot(yr_ref[0], w_ref[:half, :]) + _dot(y_sb, w_ref[half:, :])
    x = x_ref[0] + gate_ref[0] * y
    if final_norm:
        ms = jnp.mean(x * x, axis=-1, keepdims=True)
        x = x * lax.rsqrt(ms + NORM_EPS) * fg_ref[...]
    o_ref[0] = x


def _out_proj(y_rw, o_sb, sb_gate, w_bf16, x, gate, final_g, final_norm):
    b, s, d = x.shape
    half = y_rw.shape[-1]
    n_qb = s // SB_BLOCK
    sub = 8
    tm = sub * n_qb
    o_sb4 = o_sb.reshape(b, n_qb, SB_BLOCK, half)
    return pl.pallas_call(
        functools.partial(_out_proj_kernel, final_norm=final_norm),
        grid=(b, s // tm),
        in_specs=[
            pl.BlockSpec((1, tm, half), lambda i, j: (i, j, 0)),
            pl.BlockSpec((1, n_qb, sub, half), lambda i, j: (i, 0, j, 0)),
            pl.BlockSpec((1, tm, half), lambda i, j: (i, j, 0)),
            pl.BlockSpec((2 * half, d), lambda i, j: (0, 0)),
            pl.BlockSpec((1, tm, d), lambda i, j: (i, j, 0)),
            pl.BlockSpec((1, 1, d), lambda i, j: (i, 0, 0)),
            pl.BlockSpec((1, d), lambda i, j: (0, 0)),
        ],
        out_specs=pl.BlockSpec((1, tm, d), lambda i, j: (i, j, 0)),
        out_shape=jax.ShapeDtypeStruct((b, s, d), F32),
        scratch_shapes=[pltpu.VMEM((half // LANES, tm, LANES), F32)],
        compiler_params=pltpu.CompilerParams(
            dimension_semantics=("parallel", "parallel"), vmem_limit_bytes=VMEM_LIMIT_BYTES),
        name="out_proj",
    )(y_rw, o_sb4, sb_gate, w_bf16, x, gate.reshape(b, 1, d), final_g.reshape(1, d))


def _row_tile(s):
    for tm in (512, 256, 128, 64, 32, 16, 8):
        if s % tm == 0:
            return tm
    raise ValueError(f"sequence length {s} must be a multiple of 8")


def kernel(x, c, norm_g, ada_w, ada_b, w_in, w_out, tshift_mu, decay_w0, decay_w2, iclr_a0, iclr_a2,
           k_k, k_a, r_k, rwkv_ln_w, rwkv_ln_b, sb_norm_g, final_g):
    depth = norm_g.shape[0]
    b, s, d = x.shape
    assert s % SB_BLOCK == 0 and s % RWKV_CHUNK == 0
    tm = _row_tile(s)
    mod = _ada_mod(c, ada_w, ada_b)
    w_in_b = w_in.astype(BF16)
    w_out_b = w_out.astype(BF16)
    for l in range(depth):
        shift, scale, gate = mod[l, :, :d], mod[l, :, d:2 * d], mod[l, :, 2 * d:]
        rw, zz, sb_qkv, sb_gate = _in_proj(x, norm_g[l], scale, shift, w_in_b[l], tm)
        y_rw = _rwkv(rw, zz, tshift_mu[l], decay_w0[l], decay_w2[l], iclr_a0[l], iclr_a2[l],
                     k_k[l], k_a[l], r_k[l], rwkv_ln_w[l], rwkv_ln_b[l])
        o_sb = _stick_breaking(sb_qkv, sb_norm_g[l])
        x = _out_proj(y_rw, o_sb, sb_gate, w_out_b[l], x, gate, final_g, l == depth - 1)
    return x
```

```python
import functools

import jax
import jax.numpy as jnp
from jax import lax
from jax.experimental import pallas as pl
from jax.experimental.pallas import tpu as pltpu

F32 = jnp.float32
BF16 = jnp.bfloat16

HEAD_DIM = 64
LANES = 128
HEADS_PER_BLOCK = LANES // HEAD_DIM
RWKV_CHUNK = 64
RWKV_STEP_CHUNKS = 8
SB_BLOCK = 128
SB_STEP_BLOCKS = 2
SB_F32_EXP_UNDERFLOW = 106.0
DECAY_SCALE = 0.6065306597126334
DECAY_LORA = 64
ICLR_LORA = 64
NORM_EPS = 1e-6
GN_EPS = 64e-5
VMEM_LIMIT_BYTES = 56 * 1024 * 1024

NN = (((1,), (0,)), ((), ()))
NT = (((1,), (1,)), ((), ()))
TN = (((0,), (0,)), ((), ()))


def _dot(a, b, dims=NN):
    return lax.dot_general(a, b, dims, preferred_element_type=F32)


def _split2(x):
    hi = x.astype(BF16)
    lo = (x - hi.astype(F32)).astype(BF16)
    return hi, lo


def _split3(x):
    hi = x.astype(BF16)
    r1 = x - hi.astype(F32)
    mid = r1.astype(BF16)
    lo = (r1 - mid.astype(F32)).astype(BF16)
    return hi, mid, lo


def _dot_exact_rhs(x, rhs_bf16, dims=NN):
    hi, mid, lo = _split3(x)
    return _dot(hi, rhs_bf16, dims) + _dot(mid, rhs_bf16, dims) + _dot(lo, rhs_bf16, dims)


def _dot_x3(a, b_hi, b_lo):
    a_hi, a_lo = _split2(a)
    return _dot(a_hi, b_hi) + _dot(a_lo, b_hi) + _dot(a_hi, b_lo)


def _sigmoid(u):
    return 1.0 / (1.0 + jnp.exp(-u))


def _silu(u):
    return u * _sigmoid(u)


def _head_ones():
    r = lax.broadcasted_iota(jnp.int32, (LANES, LANES), 0)
    c = lax.broadcasted_iota(jnp.int32, (LANES, LANES), 1)
    return jnp.where((r >= HEAD_DIM) == (c >= HEAD_DIM), 1.0, 0.0).astype(BF16)


def _ada_kernel(c_ref, w_ref, b_ref, o_ref):
    c_act = _silu(c_ref[...])
    w_hi, w_lo = _split2(w_ref[0])
    o_ref[0] = _dot_x3(c_act, w_hi, w_lo) + b_ref[0]


def _ada_mod(c, ada_w, ada_b):
    depth, d, d3 = ada_w.shape
    b = c.shape[0]
    rows = 8
    c8 = jnp.zeros((rows, d), F32).at[:b].set(c)
    out = pl.pallas_call(
        _ada_kernel,
        grid=(depth, d3 // d),
        in_specs=[
            pl.BlockSpec((rows, d), lambda l, j: (0, 0)),
            pl.BlockSpec((1, d, d), lambda l, j: (l, 0, j)),
            pl.BlockSpec((1, 1, d), lambda l, j: (l, 0, j)),
        ],
        out_specs=pl.BlockSpec((1, rows, d), lambda l, j: (l, 0, j)),
        out_shape=jax.ShapeDtypeStruct((depth, rows, d3), F32),
        name="ada_mod",
    )(c8, ada_w, ada_b.reshape(depth, 1, d3))
    return out[:, :b]


def _in_proj_kernel(x_ref, g_ref, sc_ref, sh_ref, w_ref, rw_ref, zz_ref, sbq_ref, sbg_ref, *, widths):
    rw_w, zz_w, sbq_w, sbg_w = widths
    x = x_ref[0]
    ms = jnp.mean(x * x, axis=-1, keepdims=True)
    h = x * lax.rsqrt(ms + NORM_EPS) * g_ref[...]
    h = h * (1.0 + sc_ref[0]) + sh_ref[0]
    hb = h.astype(BF16)

    def emit(out_ref, out_col, w_col, width):
        step = 512
        for c0 in range(0, width, step):
            cw = min(step, width - c0)
            res = _dot(hb, w_ref[:, w_col + c0:w_col + c0 + cw])
            out_ref[0, :, out_col + c0:out_col + c0 + cw] = res.astype(out_ref.dtype)

    shift_rkv = rw_w - sbg_w
    emit(rw_ref, 0, 0, shift_rkv)
    emit(zz_ref, 0, shift_rkv, zz_w)
    emit(rw_ref, shift_rkv, shift_rkv + zz_w, sbg_w)
    sb0 = rw_w + zz_w
    emit(sbq_ref, 0, sb0, sbq_w)
    emit(sbg_ref, 0, sb0 + sbq_w, sbg_w)


def _in_proj(x, g, scale, shift, w_bf16, tm):
    b, s, d = x.shape
    width = d // 2
    widths = (4 * width, DECAY_LORA + ICLR_LORA, 3 * width, width)
    n_cols = w_bf16.shape[1]
    assert sum(widths) == n_cols
    return pl.pallas_call(
        functools.partial(_in_proj_kernel, widths=widths),
        grid=(b, s // tm),
        in_specs=[
            pl.BlockSpec((1, tm, d), lambda i, j: (i, j, 0)),
            pl.BlockSpec((1, d), lambda i, j: (0, 0)),
            pl.BlockSpec((1, 1, d), lambda i, j: (i, 0, 0)),
            pl.BlockSpec((1, 1, d), lambda i, j: (i, 0, 0)),
            pl.BlockSpec((d, n_cols), lambda i, j: (0, 0), pipeline_mode=pl.Buffered(1)),
        ],
        out_specs=[
            pl.BlockSpec((1, tm, widths[0]), lambda i, j: (i, j, 0)),
            pl.BlockSpec((1, tm, widths[1]), lambda i, j: (i, j, 0)),
            pl.BlockSpec((1, tm, widths[2]), lambda i, j: (i, j, 0)),
            pl.BlockSpec((1, tm, widths[3]), lambda i, j: (i, j, 0)),
        ],
        out_shape=[
            jax.ShapeDtypeStruct((b, s, widths[0]), F32),
            jax.ShapeDtypeStruct((b, s, widths[1]), F32),
            jax.ShapeDtypeStruct((b, s, widths[2]), BF16),
            jax.ShapeDtypeStruct((b, s, widths[3]), F32),
        ],
        compiler_params=pltpu.CompilerParams(
            dimension_semantics=("parallel", "parallel"), vmem_limit_bytes=VMEM_LIMIT_BYTES),
        name="in_proj",
    )(x, g.reshape(1, d), scale.reshape(b, 1, d), shift.reshape(b, 1, d), w_bf16)


def _rwkv_kernel(rkv_ref, zz_ref, g_ref, mu_ref, muz_ref, wl_ref, vec_ref, y_ref,
                 state_ref, prev_ref, prevz_ref, *, n_blocks):
    L = RWKV_CHUNK
    n_batch, rows = rkv_ref.shape[0], rkv_ref.shape[1]
    n_sub = rows // L
    width = n_blocks * LANES
    t = pl.program_id(0)

    @pl.when(t == 0)
    def _():
        state_ref[...] = jnp.zeros_like(state_ref)
        prev_ref[...] = jnp.zeros_like(prev_ref)
        prevz_ref[...] = jnp.zeros_like(prevz_ref)

    def token_shift(x, carry_ref, b, mu):
        row = lax.broadcasted_iota(jnp.int32, x.shape, 0)
        prev = jnp.where(row == 0, carry_ref[b, 7:8, :], pltpu.roll(x, 1, 0))
        carry_ref[b] = x[rows - 8:rows]
        return x + (prev - x) * mu

    k_k = vec_ref[0:1, :]
    k_a = vec_ref[1:2, :]
    r_k = vec_ref[2:3, :]
    ln_w = vec_ref[3:4, :]
    ln_b = vec_ref[4:5, :]
    w0 = vec_ref[5:6, :]
    a0 = vec_ref[6:7, :]

    lane = lax.broadcasted_iota(jnp.int32, (L, LANES), 1)
    head0 = lane < HEAD_DIM
    tr = lax.broadcasted_iota(jnp.int32, (L, 3 * L), 0)
    tc = lax.broadcasted_iota(jnp.int32, (L, 3 * L), 1)
    tc = jnp.where(tc >= 2 * L, tc - 2 * L, jnp.where(tc >= L, tc - L, tc))
    cum3 = jnp.where(tc <= tr, 1.0, 0.0).astype(BF16)
    pr = lax.broadcasted_iota(jnp.int32, (L, 2 * L), 0)
    pc = lax.broadcasted_iota(jnp.int32, (L, 2 * L), 1)
    pc = jnp.where(pc >= L, pc - L, pc)
    strict = pc < pr
    incl = pc <= pr
    eye2 = jnp.where(pc == pr, 1.0, 0.0)
    br = lax.broadcasted_iota(jnp.int32, (LANES, LANES), 0)
    bc = lax.broadcasted_iota(jnp.int32, (LANES, LANES), 1)
    same_head = (br >= HEAD_DIM) == (bc >= HEAD_DIM)

    def stacked(x):
        xb = x.astype(BF16)
        zero = jnp.zeros_like(xb)
        return jnp.concatenate([jnp.where(head0, xb, zero), jnp.where(head0, zero, xb)], axis=0)

    def head_sum(x):
        s0 = jnp.sum(jnp.where(head0, x, 0.0), axis=1, keepdims=True)
        s1 = jnp.sum(jnp.where(head0, 0.0, x), axis=1, keepdims=True)
        return jnp.where(head0, s0, s1)

    batches = range(n_batch)
    lane_z = lax.broadcasted_iota(jnp.int32, (rows, LANES), 1)
    xs, lora = [], []
    for b in batches:
        xs.append(token_shift(rkv_ref[b], prev_ref, b, mu_ref[...]))
        zs = token_shift(zz_ref[b], prevz_ref, b, muz_ref[...])
        act = jnp.where(lane_z < DECAY_LORA, jnp.tanh(zs), zs)
        a_hi, a_lo = _split2(act)
        lora.append(_dot(jnp.concatenate([a_hi, a_lo, a_hi], axis=1), wl_ref[...]))

    chains = [(b, p, q) for q in range(n_sub) for b in batches for p in range(n_blocks)]
    ids = range(len(chains))
    sls = [slice(p * LANES, (p + 1) * LANES) for _, p, _ in chains]
    rws = [slice(q * L, (q + 1) * L) for _, _, q in chains]
    r = [xs[b][rws[i], p * LANES:(p + 1) * LANES] for i, (b, p, _) in enumerate(chains)]
    k = [xs[b][rws[i], width + p * LANES:width + (p + 1) * LANES] for i, (b, p, _) in enumerate(chains)]
    v = [xs[b][rws[i], 2 * width + p * LANES:2 * width + (p + 1) * LANES] for i, (b, p, _) in enumerate(chains)]
    lw, a = [], []
    for i, (b, p, _) in enumerate(chains):
        lw.append(-DECAY_SCALE * _sigmoid(w0[:, sls[i]] + lora[b][rws[i], sls[i]]))
        a.append(_sigmoid(a0[:, sls[i]] + lora[b][rws[i], width + p * LANES:width + (p + 1) * LANES]))

    c = [_dot(cum3, jnp.concatenate(_split3(lw[i]), axis=0)) for i in ids]
    kk_raw = [k[i] * k_k[:, sls[i]] for i in ids]
    kk_ss = [head_sum(kk_raw[i] * kk_raw[i]) for i in ids]
    kmod = [k[i] * (1.0 + (a[i] - 1.0) * k_a[:, sls[i]]) for i in ids]
    bonus_dot = [head_sum(r[i] * kmod[i] * r_k[:, sls[i]]) for i in ids]

    lhs, rhs, xa_s, xr_b, v_s, upd, decay_last = [], [], [], [], [], [], []
    for i in ids:
        kk = kk_raw[i] * lax.rsqrt(jnp.maximum(kk_ss[i], 1e-24))
        bvec = kk * a[i]
        c_last = c[i][L - 1:L, :]
        g_in = jnp.exp(c[i])
        g_ex = jnp.exp(c[i] - lw[i])
        g_inv = jnp.exp(-c[i])
        g_rem = jnp.exp(c_last - c[i])
        xa = -kk * g_ex
        xr_b.append((r[i] * g_in).astype(BF16))
        xa_s.append(stacked(xa))
        v_s.append(stacked(v[i]))
        lhs.append(jnp.concatenate([xa.astype(BF16), xr_b[i]], axis=0))
        rhs.append(jnp.concatenate([stacked(bvec * g_inv), stacked(kmod[i] * g_inv)], axis=0))
        upd.append(jnp.concatenate([bvec * g_rem, kmod[i] * g_rem], axis=0).astype(BF16))
        decay_last.append(jnp.exp(c_last))

    gram = [_dot(lhs[i], rhs[i], NT) for i in ids]
    n_p = [jnp.where(strict, gram[i][:L, :2 * L], 0.0) for i in ids]
    a_ak = [jnp.where(strict, gram[i][:L, 2 * L:], 0.0).astype(BF16) for i in ids]
    a_rbk = [jnp.concatenate([jnp.where(incl, gram[i][L:, :2 * L], 0.0),
                              jnp.where(incl, gram[i][L:, 2 * L:], 0.0)], axis=1).astype(BF16) for i in ids]

    n_sq = L.bit_length() - 1
    qmat = [_dot(n_p[i].astype(BF16), stacked(n_p[i])) for i in ids]
    w1 = [_dot(a_ak[i], v_s[i]) for i in ids]
    pmat = [eye2 + n_p[i] for i in ids]
    for step in range(1, n_sq):
        q_s = [stacked(qmat[i]) for i in ids]
        if step < n_sq - 1:
            both = [_dot(jnp.concatenate([pmat[i], qmat[i]], axis=0).astype(BF16), q_s[i]) for i in ids]
            pmat = [pmat[i] + both[i][:L] for i in ids]
            qmat = [both[i][L:] for i in ids]
        else:
            pq = [_dot(pmat[i].astype(BF16), q_s[i]) for i in ids]
            pmat = [pmat[i] + pq[i] for i in ids]

    aw = [_dot(pmat[i].astype(BF16), jnp.concatenate([xa_s[i], stacked(w1[i])], axis=1)) for i in ids]
    n_state = n_batch * n_blocks
    h_t = [state_ref[j] for j in range(n_state)]
    y = []
    for q in range(n_sub):
        sub = range(q * n_state, (q + 1) * n_state)
        s9 = [_dot(jnp.concatenate([aw[i][:, :LANES].astype(BF16), xr_b[i]], axis=0),
                   h_t[i - q * n_state].astype(BF16), NT) for i in sub]
        u = [s9[j][:L] + aw[i][:, LANES:] for j, i in enumerate(sub)]
        y += [s9[j][L:] + _dot(a_rbk[i], jnp.concatenate([stacked(u[j]), v_s[i]], axis=0))
              for j, i in enumerate(sub)]
        for j, i in enumerate(sub):
            uv = jnp.concatenate([u[j], v[i]], axis=0).astype(BF16)
            h_t[j] = h_t[j] * decay_last[i] + jnp.where(same_head, _dot(uv, upd[i], TN), 0.0)
    for j in range(n_state):
        state_ref[j] = h_t[j]

    mean = [head_sum(y[i]) * (1.0 / HEAD_DIM) for i in ids]
    dlt = [y[i] - mean[i] for i in ids]
    var = [head_sum(dlt[i] * dlt[i]) * (1.0 / HEAD_DIM) for i in ids]
    for i, (b, p, _) in enumerate(chains):
        yn = dlt[i] * lax.rsqrt(var[i] + GN_EPS) * ln_w[:, sls[i]] + ln_b[:, sls[i]]
        gate = g_ref[b, rws[i], sls[i]]
        y_ref[b, rws[i], sls[i]] = ((yn + bonus_dot[i] * v[i]) * _silu(gate)).astype(y_ref.dtype)


def _rwkv(rw, zz, mu, w0, w2, a0, a2, k_k, k_a, r_k, ln_w, ln_b):
    b, s, w4 = rw.shape
    width = w4 // 4
    n_blocks = width // LANES
    L = RWKV_CHUNK
    mu_rkv = mu[:3 * width].reshape(1, 3 * width)
    mu_z = mu[3 * width:].reshape(1, DECAY_LORA + ICLR_LORA)
    w_lora = jnp.zeros((DECAY_LORA + ICLR_LORA, 2 * width), F32)
    w_lora = w_lora.at[:DECAY_LORA, :width].set(w2).at[DECAY_LORA:, width:].set(a2)
    wl_hi = w_lora.astype(BF16)
    wl_lo = (w_lora - wl_hi.astype(F32)).astype(BF16)
    wl_cat = jnp.concatenate([wl_hi, wl_hi, wl_lo], axis=0)
    vec = jnp.stack([k_k, k_a, r_k.reshape(width), ln_w, ln_b, w0, a0, jnp.zeros_like(w0)], axis=0)
    rows = L * RWKV_STEP_CHUNKS
    assert s % rows == 0
    return pl.pallas_call(
        functools.partial(_rwkv_kernel, n_blocks=n_blocks),
        grid=(s // rows,),
        in_specs=[
            pl.BlockSpec((b, rows, 3 * width), lambda j: (0, j, 0)),
            pl.BlockSpec((b, rows, LANES), lambda j: (0, j, 0)),
            pl.BlockSpec((b, rows, width), lambda j: (0, j, 3)),
            pl.BlockSpec((1, 3 * width), lambda j: (0, 0)),
            pl.BlockSpec((1, LANES), lambda j: (0, 0)),
            pl.BlockSpec((3 * LANES, 2 * width), lambda j: (0, 0)),
            pl.BlockSpec((8, width), lambda j: (0, 0)),
        ],
        out_specs=pl.BlockSpec((b, rows, width), lambda j: (0, j, 0)),
        out_shape=jax.ShapeDtypeStruct((b, s, width), BF16),
        scratch_shapes=[
            pltpu.VMEM((b * n_blocks, LANES, LANES), F32),
            pltpu.VMEM((b, 8, 3 * width), F32),
            pltpu.VMEM((b, 8, LANES), F32),
        ],
        compiler_params=pltpu.CompilerParams(
            dimension_semantics=("arbitrary",), vmem_limit_bytes=VMEM_LIMIT_BYTES),
        name="rwkv7",
    )(rw, zz, rw, mu_rkv, mu_z, wl_cat, vec)


def _sb_kernel(q_ref, k_ref, v_ref, sg_ref, o_ref, acc_ref, carry_ref, *, n_blocks):
    T = SB_BLOCK
    n_sub = q_ref.shape[1] // T
    step = pl.program_id(0)
    chains = [(h, b, slice(p * LANES, (p + 1) * LANES))
              for h in range(n_sub) for b in range(q_ref.shape[0]) for p in range(n_blocks)]
    ids = range(len(chains))
    lane = lax.broadcasted_iota(jnp.int32, (T, LANES), 1)
    head0 = lane < HEAD_DIM
    scale = HEAD_DIM ** -0.5
    qs = []
    for h, b, cols in chains:
        q = q_ref[b, h * T:(h + 1) * T, cols]
        zero = jnp.zeros_like(q)
        qs.append(jnp.concatenate([jnp.where(head0, q, zero), jnp.where(head0, zero, q)], axis=0) * scale)

    sr = lax.broadcasted_iota(jnp.int32, (T, T), 0)
    sc = lax.broadcasted_iota(jnp.int32, (T, T), 1)
    tri_ones = jnp.concatenate([jnp.where(sr > sc, 1.0, 0.0), jnp.ones((T, T), F32)], axis=1).astype(BF16)
    tri_ones2 = jnp.concatenate([tri_ones, tri_ones], axis=0)
    mr = lax.broadcasted_iota(jnp.int32, (2 * T, T), 0)
    mc = lax.broadcasted_iota(jnp.int32, (2 * T, T), 1)
    causal = mc < jnp.where(mr >= T, mr - T, mr)

    def tiles(sel, back, diag):
        starts = [pl.multiple_of((step * n_sub + h - back) * T, T) for h in range(n_sub)]
        z = {i: _dot(qs[i], k_ref[chains[i][1], pl.ds(starts[chains[i][0]], T), chains[i][2]], NT)
             for i in sel}
        neg_1m = {i: jnp.maximum(z[i], 0.0) + jnp.log(1.0 + jnp.exp(-jnp.abs(z[i]))) for i in sel}
        masked = {i: jnp.where(causal, neg_1m[i], 0.0) for i in sel} if diag else neg_1m
        sums = {i: _dot(jnp.concatenate(_split2(masked[i]), axis=1), tri_ones2) for i in sel}
        for i in sel:
            h, b, cols = chains[i]
            log_b = z[i] - neg_1m[i]
            if diag:
                attn = jnp.where(causal, jnp.exp(log_b - sums[i][:, :T]), 0.0)
                acc_ref[i] = _dot(attn.astype(BF16), v_ref[b, pl.ds(starts[h], T), cols])
                carry_ref[i] = sums[i][:, T:]
            else:
                carry = carry_ref[i]
                attn = jnp.exp(log_b - sums[i][:, :T] - carry)
                acc_ref[i] += _dot(attn.astype(BF16), v_ref[b, pl.ds(starts[h], T), cols])
                carry_ref[i] = carry + sums[i][:, T:]

    def live():
        low = carry_ref[0]
        for i in ids[1:]:
            low = jnp.minimum(low, carry_ref[i])
        return (jnp.min(low) < SB_F32_EXP_UNDERFLOW).astype(jnp.int32)

    tiles(ids, 0, True)
    n_common = step * n_sub

    def cond(state):
        it, alive = state
        return jnp.logical_and(it < n_common, alive > 0)

    def body(state):
        it, _ = state
        tiles(ids, it + 1, False)
        return it + 1, live()

    _, alive = lax.while_loop(cond, body, (jnp.int32(0), live()))

    for extra in range(1, n_sub):
        @pl.when(alive > 0)
        def _():
            tiles([i for i in ids if chains[i][0] >= extra], n_common + extra, False)

    head_ones = _head_ones()
    for i, (h, b, cols) in enumerate(chains):
        acc = acc_ref[i]
        o = jnp.where(head0, acc[:T], acc[T:])
        ms = _dot_exact_rhs(o * o, head_ones) * (1.0 / HEAD_DIM)
        o_ref[b, h * T:(h + 1) * T, cols] = o * lax.rsqrt(ms + NORM_EPS) * sg_ref[:, cols]


def _stick_breaking(qkv, sb_g):
    b, s, w3 = qkv.shape
    width = w3 // 3
    n_blocks = width // LANES
    T = SB_BLOCK
    resident = pl.Buffered(1)
    rows = T * SB_STEP_BLOCKS
    assert s % rows == 0
    n_chains = SB_STEP_BLOCKS * b * n_blocks
    return pl.pallas_call(
        functools.partial(_sb_kernel, n_blocks=n_blocks),
        grid=(s // rows,),
        in_specs=[
            pl.BlockSpec((b, rows, width), lambda j: (0, j, 0)),
            pl.BlockSpec((b, s, width), lambda j: (0, 0, 1), pipeline_mode=resident),
            pl.BlockSpec((b, s, width), lambda j: (0, 0, 2), pipeline_mode=resident),
            pl.BlockSpec((1, width), lambda j: (0, 0)),
        ],
        out_specs=pl.BlockSpec((b, rows, width), lambda j: (0, j, 0)),
        out_shape=jax.ShapeDtypeStruct((b, s, width), F32),
        scratch_shapes=[
            pltpu.VMEM((n_chains, 2 * T, LANES), F32),
            pltpu.VMEM((n_chains, 2 * T, T), F32),
        ],
        compiler_params=pltpu.CompilerParams(
            dimension_semantics=("arbitrary",), vmem_limit_bytes=VMEM_LIMIT_BYTES),
        name="stick_breaking",
    )(qkv, qkv, qkv, sb_g.reshape(1, width))


def _out_proj_kernel(yr_ref, os_ref, sg_ref, w_ref, x_ref, gate_ref, fg_ref, o_ref, perm_ref, *, final_norm):
    half = yr_ref.shape[-1]
    n_qb, sub = os_ref.shape[1], os_ref.shape[2]
    o_tok = os_ref[0].reshape(n_qb * sub, half)
    cols = []
    for cb in range(half // LANES):
        perm_ref[cb] = o_tok[:, cb * LANES:(cb + 1) * LANES]
        cols.append(jnp.concatenate(
            [perm_ref[cb, pl.ds(ii, n_qb, stride=sub), :] for ii in range(sub)], axis=0))
    o_rows = jnp.concatenate(cols, axis=1)
    y_sb = (o_rows * _silu(sg_ref[0])).astype(BF16)
    y = _dot(yr_ref[0], w_ref[:half, :]) + _dot(y_sb, w_ref[half:, :])
    x = x_ref[0] + gate_ref[0] * y
    if final_norm:
        ms = jnp.mean(x * x, axis=-1, keepdims=True)
        x = x * lax.rsqrt(ms + NORM_EPS) * fg_ref[...]
    o_ref[0] = x


def _out_proj(y_rw, o_sb, sb_gate, w_bf16, x, gate, final_g, final_norm):
    b, s, d = x.shape
    half = y_rw.shape[-1]
    n_qb = s // SB_BLOCK
    sub = 8
    tm = sub * n_qb
    o_sb4 = o_sb.reshape(b, n_qb, SB_BLOCK, half)
    return pl.pallas_call(
        functools.partial(_out_proj_kernel, final_norm=final_norm),
        grid=(b, s // tm),
        in_specs=[
            pl.BlockSpec((1, tm, half), lambda i, j: (i, j, 0)),
            pl.BlockSpec((1, n_qb, sub, half), lambda i, j: (i, 0, j, 0)),
            pl.BlockSpec((1, tm, half), lambda i, j: (i, j, 0)),
            pl.BlockSpec((2 * half, d), lambda i, j: (0, 0)),
            pl.BlockSpec((1, tm, d), lambda i, j: (i, j, 0)),
            pl.BlockSpec((1, 1, d), lambda i, j: (i, 0, 0)),
            pl.BlockSpec((1, d), lambda i, j: (0, 0)),
        ],
        out_specs=pl.BlockSpec((1, tm, d), lambda i, j: (i, j, 0)),
        out_shape=jax.ShapeDtypeStruct((b, s, d), F32),
        scratch_shapes=[pltpu.VMEM((half // LANES, tm, LANES), F32)],
        compiler_params=pltpu.CompilerParams(
            dimension_semantics=("parallel", "parallel"), vmem_limit_bytes=VMEM_LIMIT_BYTES),
        name="out_proj",
    )(y_rw, o_sb4, sb_gate, w_bf16, x, gate.reshape(b, 1, d), final_g.reshape(1, d))


def _row_tile(s):
    for tm in (1024, 512, 256, 128, 64, 32, 16, 8):
        if s % tm == 0:
            return tm
    raise ValueError(f"sequence length {s} must be a multiple of 8")


def kernel(x, c, norm_g, ada_w, ada_b, w_in, w_out, tshift_mu, decay_w0, decay_w2, iclr_a0, iclr_a2,
           k_k, k_a, r_k, rwkv_ln_w, rwkv_ln_b, sb_norm_g, final_g):
    depth = norm_g.shape[0]
    b, s, d = x.shape
    assert s % SB_BLOCK == 0 and s % RWKV_CHUNK == 0
    tm = _row_tile(s)
    mod = _ada_mod(c, ada_w, ada_b)
    w_in_b = w_in.astype(BF16)
    w_out_b = w_out.astype(BF16)
    for l in range(depth):
        shift, scale, gate = mod[l, :, :d], mod[l, :, d:2 * d], mod[l, :, 2 * d:]
        rw, zz, sb_qkv, sb_gate = _in_proj(x, norm_g[l], scale, shift, w_in_b[l], tm)
        y_rw = _rwkv(rw, zz, tshift_mu[l], decay_w0[l], decay_w2[l], iclr_a0[l], iclr_a2[l],
                     k_k[l], k_a[l], r_k[l], rwkv_ln_w[l], rwkv_ln_b[l])
        o_sb = _stick_breaking(sb_qkv, sb_norm_g[l])
        x = _out_proj(y_rw, o_sb, sb_gate, w_out_b[l], x, gate, final_g, l == depth - 1)
    return x
```
